```python
import math
import jax
import jax.numpy as jnp
from jax import lax
import numpy as np

D_MODEL = 1024
BATCH = 8
SEQ = 4096
DEPTH = 2

GRID_W = 64
CTX_LEN = 256
N_MIXERS = 2
N_SSD_LAYERS = (DEPTH + N_MIXERS - 1) // N_MIXERS
N_GM_LAYERS = DEPTH // N_MIXERS
N_MOD = 9
MACARON_W = 0.5
EPS = 1e-6

FFN_DIM = 2816

SSD_INNER = 2 * D_MODEL
SSD_HEAD_DIM = 64
SSD_HEADS = SSD_INNER // SSD_HEAD_DIM
SSD_GROUPS = 8
SSD_HPG = SSD_HEADS // SSD_GROUPS
SSD_STATE = 128
SSD_CONV = 5
SSD_CHUNK = 128
SSD_CONV_DIM = SSD_INNER + 2 * SSD_GROUPS * SSD_STATE
SSD_PROJ = SSD_INNER + SSD_CONV_DIM + 2 * SSD_HEADS

GM_CHUNK = 128
GM_INNER = 2 * D_MODEL
GM_GROUPS = 8
GM_GROUP_DIM = GM_INNER // GM_GROUPS

kernel_name = 'hybrid_ssd_gmlp_macaron_dit_block'


def rms_norm(x, g):
    xf = x.astype(jnp.float32)
    y = xf * lax.rsqrt(jnp.mean(xf * xf, axis=-1, keepdims=True) + EPS)
    return (y * g.astype(jnp.float32)).astype(x.dtype)


def layer_norm(x, g, b):
    xf = x.astype(jnp.float32)
    mu = jnp.mean(xf, axis=-1, keepdims=True)
    var = jnp.mean(jnp.square(xf - mu), axis=-1, keepdims=True)
    y = (xf - mu) * lax.rsqrt(var + EPS) * g.astype(jnp.float32) + b.astype(jnp.float32)
    return y.astype(x.dtype)


def sublayer_in(h, g_pre, shift, scale):
    return rms_norm(h, g_pre) * (1 + scale) + shift


def sublayer_out(h, y, g_post, gate, weight):
    return h + weight * gate * rms_norm(y, g_post)


def swiglu(h, w_in, w_out):
    gate, up = jnp.split(h @ w_in, 2, axis=-1)
    return (jax.nn.silu(gate) * up) @ w_out


def depthwise_conv(u, w, b):
    y = lax.conv_general_dilated(
        u, w[:, None, :].astype(u.dtype), window_strides=(1,),
        padding=[(SSD_CONV // 2, SSD_CONV // 2)],
        dimension_numbers=('NWC', 'WIO', 'NWC'),
        feature_group_count=u.shape[-1])
    return y + b


def ssd_chunked(xh, dt, A, Bm, Cm, h0):
    b, l, g, k, p = xh.shape
    n = Bm.shape[-1]
    q = SSD_CHUNK
    nc = l // q
    xc = xh.reshape(b, nc, q, g, k, p)
    dtc = dt.reshape(b, nc, q, g, k)
    Bc = Bm.reshape(b, nc, q, g, n)
    Cc = Cm.reshape(b, nc, q, g, n)
    a_cum = jnp.cumsum(dtc * A, axis=2)
    xdt = xc * dtc[..., None]
    tri = jnp.tril(jnp.ones((q, q), dtype=bool))[None, None, :, :, None, None]
    seg = a_cum[:, :, :, None] - a_cum[:, :, None, :]
    decay = jnp.exp(jnp.where(tri, seg, -jnp.inf))
    cb = jnp.einsum('bcign,bcjgn->bcijg', Cc, Bc)
    y_diag = jnp.einsum('bcijg,bcijgk,bcjgkp->bcigkp', cb, decay, xdt)
    decay_end = jnp.exp(a_cum[:, :, -1:] - a_cum)
    states = jnp.einsum('bcqgn,bcqgk,bcqgkp->bcgkpn', Bc, decay_end, xdt)
    chunk_decay = jnp.exp(a_cum[:, :, -1])

    def step(h, inp):
        s, d = inp
        return h * d[..., None, None] + s, h

    h_last, h_start = lax.scan(step, h0, (jnp.moveaxis(states, 1, 0), jnp.moveaxis(chunk_decay, 1, 0)))
    h_start = jnp.moveaxis(h_start, 0, 1)
    y_off = jnp.einsum('bcign,bcgkpn,bcigk->bcigkp', Cc, h_start, jnp.exp(a_cum))
    return (y_diag + y_off).reshape(b, l, g, k, p), h_last


def ssd_branch(u, h0f, h0b, w_in, conv_w, conv_b, dt_bias, A_log, D_skip, norm_g, w_out):
    bsz, l, _ = u.shape
    proj = u @ w_in
    z = proj[..., :SSD_INNER]
    xbc = proj[..., SSD_INNER:SSD_INNER + SSD_CONV_DIM]
    dt_raw = proj[..., SSD_INNER + SSD_CONV_DIM:]
    xbc = jax.nn.silu(depthwise_conv(xbc, conv_w, conv_b)).astype(jnp.float32)
    gn = SSD_GROUPS * SSD_STATE
    xs = xbc[..., :SSD_INNER].reshape(bsz, l, SSD_GROUPS, SSD_HPG, SSD_HEAD_DIM)
    Bm = xbc[..., SSD_INNER:SSD_INNER + gn].reshape(bsz, l, SSD_GROUPS, SSD_STATE)
    Cm = xbc[..., SSD_INNER + gn:].reshape(bsz, l, SSD_GROUPS, SSD_STATE)
    dt = jax.nn.softplus(dt_raw.astype(jnp.float32).reshape(bsz, l, 2, SSD_GROUPS, SSD_HPG)
                         + dt_bias.astype(jnp.float32).reshape(2, SSD_GROUPS, SSD_HPG))
    A = -jnp.exp(A_log.astype(jnp.float32)).reshape(2, SSD_GROUPS, SSD_HPG)
    rev = lambda t: jnp.flip(t, axis=1)
    yf, hf = ssd_chunked(xs, dt[:, :, 0], A[0], Bm, Cm, h0f)
    yb, hb = ssd_chunked(rev(xs), rev(dt[:, :, 1]), A[1], rev(Bm), rev(Cm), h0b)
    y = yf + rev(yb) + D_skip.astype(jnp.float32).reshape(SSD_GROUPS, SSD_HPG, 1) * xs
    y = y.reshape(bsz, l, SSD_INNER) * jax.nn.silu(z.astype(jnp.float32))
    y = y.reshape(bsz, l, SSD_GROUPS, SSD_INNER // SSD_GROUPS)
    y = y * lax.rsqrt(jnp.mean(y * y, axis=-1, keepdims=True) + EPS)
    y = y.reshape(bsz, l, SSD_INNER) * norm_g.astype(jnp.float32)
    return y.astype(u.dtype) @ w_out, hf, hb


def gmlp_branch(u, w_in, v_g, v_b, w_s, b_s, w_out):
    bsz, l, _ = u.shape
    gu, gv = jnp.split(jax.nn.gelu(u @ w_in), 2, axis=-1)
    gv = layer_norm(gv, v_g, v_b).reshape(bsz, l // GM_CHUNK, GM_CHUNK, GM_GROUPS, GM_GROUP_DIM)
    s = jnp.einsum('gij,bcjgd->bcigd', w_s, gv) + b_s.T[:, :, None]
    return (gu * s.reshape(bsz, l, GM_INNER)) @ w_out


def setup_inputs(seed: int = 0) -> dict:
    key = jax.random.key(seed)
    ks = jax.random.split(key, 24)
    D = D_MODEL

    def nrm(k, shape, scale):
        return jax.random.normal(k, shape, jnp.float32) * scale

    lo, hi = math.log(1e-3), math.log(1e-1)
    dt0 = jnp.exp(jax.random.uniform(ks[12], (N_SSD_LAYERS, 2, SSD_HEADS), jnp.float32, lo, hi))
    return {
        'x': nrm(ks[0], (BATCH, SEQ, D), 1.0),
        'c': nrm(ks[1], (BATCH, D), 1.0),
        'ctx': nrm(ks[2], (BATCH, CTX_LEN, D), 1.0),
        'c_ctx': nrm(ks[3], (D,), 1.0),
        'ada_w': nrm(ks[4], (DEPTH, D, N_MOD * D), 0.5 * D ** -0.5),
        'ada_b': nrm(ks[5], (DEPTH, N_MOD * D), 0.02),
        'norm_g': 1.0 + nrm(ks[6], (DEPTH, 6, D), 0.02),
        'ffn_w_in': nrm(ks[7], (DEPTH, 2, D, 2 * FFN_DIM), D ** -0.5),
        'ffn_w_out': nrm(ks[8], (DEPTH, 2, FFN_DIM, D), FFN_DIM ** -0.5),
        'ssd_w_in': nrm(ks[9], (N_SSD_LAYERS, D, SSD_PROJ), D ** -0.5),
        'ssd_conv_w': nrm(ks[10], (N_SSD_LAYERS, SSD_CONV, SSD_CONV_DIM), SSD_CONV ** -0.5),
        'ssd_conv_b': nrm(ks[11], (N_SSD_LAYERS, SSD_CONV_DIM), 0.02),
        'ssd_dt_bias': dt0 + jnp.log(-jnp.expm1(-dt0)),
        'ssd_A_log': jnp.log(jax.random.uniform(ks[13], (N_SSD_LAYERS, 2, SSD_HEADS), jnp.float32, 1.0, 16.0)),
        'ssd_D': 1.0 + nrm(ks[14], (N_SSD_LAYERS, SSD_HEADS), 0.02),
        'ssd_norm_g': 1.0 + nrm(ks[15], (N_SSD_LAYERS, SSD_INNER), 0.02),
        'ssd_w_out': nrm(ks[16], (N_SSD_LAYERS, SSD_INNER, D), SSD_INNER ** -0.5),
        'gm_w_in': nrm(ks[17], (N_GM_LAYERS, D, 2 * GM_INNER), D ** -0.5),
        'gm_v_g': 1.0 + nrm(ks[18], (N_GM_LAYERS, GM_INNER), 0.02),
        'gm_v_b': nrm(ks[19], (N_GM_LAYERS, GM_INNER), 0.02),
        'gm_w_s': nrm(ks[20], (N_GM_LAYERS, GM_GROUPS, GM_CHUNK, GM_CHUNK), GM_CHUNK ** -0.5),
        'gm_b_s': 1.0 + nrm(ks[21], (N_GM_LAYERS, GM_GROUPS, GM_CHUNK), 0.02),
        'gm_w_out': nrm(ks[22], (N_GM_LAYERS, GM_INNER, D), GM_INNER ** -0.5),
    }


def reference(x, c, ctx, c_ctx, ada_w, ada_b, norm_g, ffn_w_in, ffn_w_out,
              ssd_w_in, ssd_conv_w, ssd_conv_b, ssd_dt_bias, ssd_A_log, ssd_D, ssd_norm_g, ssd_w_out,
              gm_w_in, gm_v_g, gm_v_b, gm_w_s, gm_b_s, gm_w_out):
    bsz = x.shape[0]
    silu_c = jax.nn.silu(c)
    silu_cc = jax.nn.silu(c_ctx)
    for i in range(DEPTH):
        use_ssd = (i % N_MIXERS) == 0
        j = i // N_MIXERS
        last = i == DEPTH - 1
        ctx_needed = (not last) or use_ssd
        ctx_full = not last
        mx = jnp.split((silu_c @ ada_w[i] + ada_b[i])[:, None, :], N_MOD, axis=-1)
        mc = jnp.split(silu_cc @ ada_w[i] + ada_b[i], N_MOD, axis=-1)
        g = norm_g[i]

        x = sublayer_out(x, swiglu(sublayer_in(x, g[0], mx[0], mx[1]), ffn_w_in[i, 0], ffn_w_out[i, 0]),
                         g[1], mx[2], MACARON_W)
        if ctx_needed:
            ctx = sublayer_out(ctx, swiglu(sublayer_in(ctx, g[0], mc[0], mc[1]), ffn_w_in[i, 0], ffn_w_out[i, 0]),
                               g[1], mc[2], MACARON_W)

        xm = sublayer_in(x, g[2], mx[3], mx[4])
        if use_ssd:
            cm = sublayer_in(ctx, g[2], mc[3], mc[4])
            h0 = jnp.zeros((bsz, SSD_GROUPS, SSD_HPG, SSD_HEAD_DIM, SSD_STATE), jnp.float32)
            y_ctx, hf, hb = ssd_branch(cm, h0, h0, ssd_w_in[j], ssd_conv_w[j], ssd_conv_b[j], ssd_dt_bias[j],
                                       ssd_A_log[j], ssd_D[j], ssd_norm_g[j], ssd_w_out[j])
            y_x, _, _ = ssd_branch(xm, hf, hb, ssd_w_in[j], ssd_conv_w[j], ssd_conv_b[j], ssd_dt_bias[j],
                                   ssd_A_log[j], ssd_D[j], ssd_norm_g[j], ssd_w_out[j])
        else:
            y_x = gmlp_branch(xm, gm_w_in[j], gm_v_g[j], gm_v_b[j], gm_w_s[j], gm_b_s[j], gm_w_out[j])
            if ctx_full:
                cm = sublayer_in(ctx, g[2], mc[3], mc[4])
                y_ctx = gmlp_branch(cm, gm_w_in[j], gm_v_g[j], gm_v_b[j], gm_w_s[j], gm_b_s[j], gm_w_out[j])
        x = sublayer_out(x, y_x, g[3], mx[5], 1.0)

        x = sublayer_out(x, swiglu(sublayer_in(x, g[4], mx[6], mx[7]), ffn_w_in[i, 1], ffn_w_out[i, 1]),
                         g[5], mx[8], MACARON_W)
        if ctx_full:
            ctx = sublayer_out(ctx, y_ctx, g[3], mc[5], 1.0)
            ctx = sublayer_out(ctx, swiglu(sublayer_in(ctx, g[4], mc[6], mc[7]), ffn_w_in[i, 1], ffn_w_out[i, 1]),
                               g[5], mc[8], MACARON_W)
    return x
```

```python
import functools

import jax
import jax.numpy as jnp
from jax import lax
from jax.experimental import pallas as pl
from jax.experimental.pallas import tpu as pltpu

F32 = jnp.float32
BF16 = jnp.bfloat16

D_MODEL = 1024
DEPTH = 2
N_MOD = 9
MACARON_W = 0.5
EPS = 1e-6
FFN_DIM = 2816

SSD_INNER = 2048
SSD_HEAD_DIM = 64
SSD_HEADS = 32
SSD_GROUPS = 8
SSD_HPG = 4
SSD_STATE = 128
SSD_CONV = 5
SSD_CHUNK = 128
SSD_GN = SSD_GROUPS * SSD_STATE
SSD_CONV_DIM = SSD_INNER + 2 * SSD_GN

GM_CHUNK = 128
GM_INNER = 2048
GM_GROUPS = 8
GM_GROUP_DIM = 256

MXU_TILE = 256
HALO = 16
VMEM_LIMIT = 56 * 1024 * 1024
MOD_ROWS = 16


def _cparams(*sem):
    return pltpu.CompilerParams(dimension_semantics=sem, vmem_limit_bytes=VMEM_LIMIT)


def _rms(x, g):
    return x * lax.rsqrt(jnp.mean(x * x, axis=-1, keepdims=True) + EPS) * g


def _silu(x):
    return x * jax.nn.sigmoid(x)


def _const_spec(shape):
    nd = len(shape)
    return pl.BlockSpec(shape, lambda *_: (0,) * nd)


def _ada_kernel(c_ref, w_ref, b_ref, o_ref):
    cs = _silu(c_ref[...]).astype(BF16)
    o_ref[...] = jnp.dot(cs, w_ref[...].astype(BF16), preferred_element_type=F32) + b_ref[...]


def _ada_mod(cvec, ada_w, ada_b):
    nd = N_MOD * D_MODEL
    bn = 1024
    return pl.pallas_call(
        _ada_kernel,
        grid=(DEPTH, nd // bn),
        in_specs=[
            pl.BlockSpec((MOD_ROWS, D_MODEL), lambda i, j: (0, 0)),
            pl.BlockSpec((None, D_MODEL, bn), lambda i, j: (i, 0, j)),
            pl.BlockSpec((None, 1, bn), lambda i, j: (i, 0, j)),
        ],
        out_specs=pl.BlockSpec((None, MOD_ROWS, bn), lambda i, j: (i, 0, j)),
        out_shape=jax.ShapeDtypeStruct((DEPTH, MOD_ROWS, nd), F32),
        compiler_params=_cparams("parallel", "parallel"),
        name="ada_mod",
    )(cvec, ada_w, ada_b.reshape(DEPTH, 1, nd))


def _ffn_kernel(h_ref, mod_ref, g_ref, wg_ref, wu_ref, wo_ref, o_ref, a_scr, *, s):
    h = h_ref[0]
    shift = mod_ref[0, 3 * s:3 * s + 1, :]
    scale = mod_ref[0, 3 * s + 1:3 * s + 2, :]
    gate = mod_ref[0, 3 * s + 2:3 * s + 3, :]
    u = (_rms(h, g_ref[2 * s:2 * s + 1, :]) * (1.0 + scale) + shift).astype(BF16)
    for c in range(FFN_DIM // MXU_TILE):
        cols = slice(c * MXU_TILE, (c + 1) * MXU_TILE)
        hg = jnp.dot(u, wg_ref[:, cols], preferred_element_type=F32)
        hu = jnp.dot(u, wu_ref[:, cols], preferred_element_type=F32)
        a_scr[:, cols] = (_silu(hg) * hu).astype(BF16)
    y = jnp.dot(a_scr[...], wo_ref[...], preferred_element_type=F32)
    o_ref[0] = h + (MACARON_W * gate) * _rms(y, g_ref[2 * s + 1:2 * s + 2, :])


def _ffn(h, mod, g, w_in, w_out, *, s, tm, mod_row):
    bsz, l, _ = h.shape
    return pl.pallas_call(
        functools.partial(_ffn_kernel, s=s),
        grid=(bsz, l // tm),
        in_specs=[
            pl.BlockSpec((1, tm, D_MODEL), lambda b, t: (b, t, 0)),
            pl.BlockSpec((1, N_MOD, D_MODEL), lambda b, t: (mod_row(b), 0, 0)),
            _const_spec((6, D_MODEL)),
            pl.BlockSpec((D_MODEL, FFN_DIM), lambda b, t: (0, 0)),
            pl.BlockSpec((D_MODEL, FFN_DIM), lambda b, t: (0, 1)),
            _const_spec((FFN_DIM, D_MODEL)),
        ],
        out_specs=pl.BlockSpec((1, tm, D_MODEL), lambda b, t: (b, t, 0)),
        out_shape=jax.ShapeDtypeStruct(h.shape, F32),
        scratch_shapes=[pltpu.VMEM((tm, FFN_DIM), BF16)],
        compiler_params=_cparams("parallel", "parallel"),
        name="ffn",
    )(h, mod, g, w_in, w_in, w_out)


def _ssd_in_kernel(xm_ref, xp_ref, xn_ref, mod_ref, g_ref, wz_ref, wx_ref, wdt_ref, cw_ref, cb_ref,
                   dtb_ref, z_ref, xbc_ref, dt_ref, u_scr, p_scr, *, tm):
    t = pl.program_id(1)
    nt = pl.num_programs(1)
    shift = mod_ref[0, 3:4, :]
    scale = mod_ref[0, 4:5, :]
    g_pre = g_ref[2:3, :]

    def prep(h):
        return (_rms(h, g_pre) * (1.0 + scale) + shift).astype(BF16)

    u_scr[0:HALO, :] = prep(xp_ref[0])
    u_scr[HALO:HALO + tm, :] = prep(xm_ref[0])
    u_scr[HALO + tm:, :] = prep(xn_ref[0])
    um = u_scr[HALO:HALO + tm, :]
    z_ref[0] = jnp.dot(um, wz_ref[...], preferred_element_type=F32).astype(BF16)
    dt_raw = jnp.dot(um, wdt_ref[...], preferred_element_type=F32) + dtb_ref[...]
    dt_ref[0] = jnp.maximum(dt_raw, 0.0) + jnp.log1p(jnp.exp(-jnp.abs(dt_raw)))
    p_scr[...] = jnp.dot(u_scr[...], wx_ref[...], preferred_element_type=F32)

    @pl.when(t == 0)
    def _():
        p_scr[0:HALO, :] = jnp.zeros((HALO, SSD_CONV_DIM), F32)

    @pl.when(t == nt - 1)
    def _():
        p_scr[HALO + tm:, :] = jnp.zeros((HALO, SSD_CONV_DIM), F32)

    cblk = 512
    for cidx in range(SSD_CONV_DIM // cblk):
        cols = slice(cidx * cblk, (cidx + 1) * cblk)
        acc = cb_ref[:, cols]
        for k in range(SSD_CONV):
            r0 = HALO - SSD_CONV // 2 + k
            acc = acc + cw_ref[k:k + 1, cols] * p_scr[r0:r0 + tm, cols]
        xbc_ref[0, :, cols] = _silu(acc).astype(BF16)


def _ssd_in(h, mod, g, wz, wx, wdt, conv_w, conv_b, dt_bias, *, tm, mod_row):
    bsz, l, _ = h.shape
    nh = tm // HALO
    last = l // HALO - 1
    return pl.pallas_call(
        functools.partial(_ssd_in_kernel, tm=tm),
        grid=(bsz, l // tm),
        in_specs=[
            pl.BlockSpec((1, tm, D_MODEL), lambda b, t: (b, t, 0)),
            pl.BlockSpec((1, HALO, D_MODEL), lambda b, t: (b, jnp.maximum(t * nh - 1, 0), 0)),
            pl.BlockSpec((1, HALO, D_MODEL), lambda b, t: (b, jnp.minimum((t + 1) * nh, last), 0)),
            pl.BlockSpec((1, N_MOD, D_MODEL), lambda b, t: (mod_row(b), 0, 0)),
            _const_spec((6, D_MODEL)),
            _const_spec((D_MODEL, SSD_INNER)),
            _const_spec((D_MODEL, SSD_CONV_DIM)),
            _const_spec((D_MODEL, 2 * SSD_HEADS)),
            _const_spec((SSD_CONV, SSD_CONV_DIM)),
            _const_spec((1, SSD_CONV_DIM)),
            _const_spec((1, 2 * SSD_HEADS)),
        ],
        out_specs=[
            pl.BlockSpec((1, tm, SSD_INNER), lambda b, t: (b, t, 0)),
            pl.BlockSpec((1, tm, SSD_CONV_DIM), lambda b, t: (b, t, 0)),
            pl.BlockSpec((1, tm, 2 * SSD_HEADS), lambda b, t: (b, t, 0)),
        ],
        out_shape=[
            jax.ShapeDtypeStruct((bsz, l, SSD_INNER), BF16),
            jax.ShapeDtypeStruct((bsz, l, SSD_CONV_DIM), BF16),
            jax.ShapeDtypeStruct((bsz, l, 2 * SSD_HEADS), F32),
        ],
        scratch_shapes=[
            pltpu.VMEM((tm + 2 * HALO, D_MODEL), BF16),
            pltpu.VMEM((tm + 2 * HALO, SSD_CONV_DIM), F32),
        ],
        compiler_params=_cparams("parallel", "parallel"),
        name="ssd_in",
    )(h, h, h, mod, g, wz, wx, wdt, conv_w, conv_b.reshape(1, -1), dt_bias.reshape(1, -1))


_NT = (((1,), (1,)), ((), ()))
_TN = (((0,), (0,)), ((), ()))


def _scan_dir(d, x_ref, dt_ref, alog_ref, dskip_ref, y_ref, h_ref):
    q = SSD_CHUNK
    dt = dt_ref[0][:, SSD_HEADS * d:SSD_HEADS * (d + 1)]
    a_neg = -jnp.exp(alog_ref[d:d + 1, :])
    row = lax.broadcasted_iota(jnp.int32, (q, q), 0)
    col = lax.broadcasted_iota(jnp.int32, (q, q), 1)
    tri = (row >= col) if d == 0 else (row <= col)
    a = jnp.dot(tri.astype(F32), dt * a_neg, preferred_element_type=F32,
                precision=lax.Precision.HIGHEST)
    eye = (lax.broadcasted_iota(jnp.int32, (SSD_HEADS, SSD_HEADS), 0)
           == lax.broadcasted_iota(jnp.int32, (SSD_HEADS, SSD_HEADS), 1)).astype(F32)
    a_t = lax.dot_general(eye, a, _NT, preferred_element_type=F32,
                          precision=lax.Precision.HIGHEST)
    dt_t = lax.dot_general(eye, dt, _NT, preferred_element_type=F32,
                           precision=lax.Precision.HIGHEST)
    a_end = a[q - 1:q, :] if d == 0 else a[0:1, :]
    w_end = dt * jnp.exp(a_end - a)
    e_in = jnp.exp(a)
    c_dec = jnp.exp(a_end)
    low_half = lax.broadcasted_iota(jnp.int32, (q, 2 * SSD_HEAD_DIM), 1) < SSD_HEAD_DIM

    def per_head_lanes(v, g):
        pairs = [jnp.where(low_half, v[:, hd:hd + 1], v[:, hd + 1:hd + 2])
                 for hd in (SSD_HPG * g, SSD_HPG * g + 2)]
        return jnp.concatenate(pairs, axis=1)

    gw = SSD_HPG * SSD_HEAD_DIM
    for g in range(SSD_GROUPS):
        gs = slice(gw * g, gw * (g + 1))
        b_g = x_ref[0, :, SSD_INNER + SSD_STATE * g:SSD_INNER + SSD_STATE * (g + 1)]
        c_g = x_ref[0, :, SSD_INNER + SSD_GN + SSD_STATE * g:SSD_INNER + SSD_GN + SSD_STATE * (g + 1)]
        x_g = x_ref[0, :, gs]
        h_g = h_ref[0, d, gs, :]
        cb = lax.dot_general(c_g, b_g, _NT, preferred_element_type=F32)
        y_g = lax.dot_general(c_g, h_g.astype(BF16), _NT, preferred_element_type=F32) \
            * per_head_lanes(e_in, g)
        if d == 0:
            y_g = y_g + dskip_ref[:, gs] * x_g.astype(F32)
        ms = []
        for k in range(SSD_HPG):
            hd = SSD_HPG * g + k
            seg = a[:, hd:hd + 1] - a_t[hd:hd + 1, :]
            decay = jnp.exp(jnp.where(tri, seg, -jnp.inf))
            ms.append((cb * decay * dt_t[hd:hd + 1, :]).astype(BF16))
        y_pairs = []
        for pr in range(SSD_HPG // 2):
            x_pair = x_g[:, 2 * SSD_HEAD_DIM * pr:2 * SSD_HEAD_DIM * (pr + 1)]
            zero = jnp.zeros_like(x_pair)
            x_bd = jnp.concatenate([jnp.where(low_half, x_pair, zero),
                                    jnp.where(low_half, zero, x_pair)], axis=0)
            m_pair = jnp.concatenate([ms[2 * pr], ms[2 * pr + 1]], axis=1)
            y_pairs.append(jnp.dot(m_pair, x_bd, preferred_element_type=F32))
        y_ref[0, :, gs] = (y_g + jnp.concatenate(y_pairs, axis=1)).astype(BF16)
        xw_t = (x_g.astype(F32) * per_head_lanes(w_end, g)).T.astype(BF16)
        s_g = jnp.dot(xw_t, b_g, preferred_element_type=F32)
        dec_rows = jnp.concatenate(
            [jnp.broadcast_to(c_dec[:, SSD_HPG * g + k:SSD_HPG * g + k + 1], (SSD_HEAD_DIM, SSD_STATE))
             for k in range(SSD_HPG)], axis=0)
        h_ref[0, d, gs, :] = h_g * dec_rows + s_g


def _ssd_scan_kernel(xf_ref, xb_ref, dtf_ref, dtb_ref, alog_ref, dskip_ref, h0_ref,
                     yf_ref, yb_ref, h_ref):
    @pl.when(pl.program_id(1) == 0)
    def _():
        h_ref[...] = h0_ref[...]

    _scan_dir(0, xf_ref, dtf_ref, alog_ref, dskip_ref, yf_ref, h_ref)
    _scan_dir(1, xb_ref, dtb_ref, alog_ref, dskip_ref, yb_ref, h_ref)


def _ssd_scan(xbc, dt, a_log, d_skip, h0):
    bsz, l, _ = xbc.shape
    nc = l // SSD_CHUNK
    fwd = lambda b, c: (b, c, 0)
    bwd = lambda b, c: (b, nc - 1 - c, 0)
    st_spec = pl.BlockSpec((1, 2, SSD_INNER, SSD_STATE), lambda b, c: (b, 0, 0, 0))
    return pl.pallas_call(
        _ssd_scan_kernel,
        grid=(bsz, nc),
        in_specs=[
            pl.BlockSpec((1, SSD_CHUNK, SSD_CONV_DIM), fwd),
            pl.BlockSpec((1, SSD_CHUNK, SSD_CONV_DIM), bwd),
            pl.BlockSpec((1, SSD_CHUNK, 2 * SSD_HEADS), fwd),
            pl.BlockSpec((1, SSD_CHUNK, 2 * SSD_HEADS), bwd),
            _const_spec((2, SSD_HEADS)),
            _const_spec((1, SSD_INNER)),
            st_spec,
        ],
        out_specs=[
            pl.BlockSpec((1, SSD_CHUNK, SSD_INNER), fwd),
            pl.BlockSpec((1, SSD_CHUNK, SSD_INNER), bwd),
            st_spec,
        ],
        out_shape=[
            jax.ShapeDtypeStruct((bsz, l, SSD_INNER), BF16),
            jax.ShapeDtypeStruct((bsz, l, SSD_INNER), BF16),
            jax.ShapeDtypeStruct(h0.shape, F32),
        ],
        compiler_params=_cparams("parallel", "arbitrary"),
        name="ssd_scan",
    )(xbc, xbc, dt, dt, a_log, d_skip, h0)


def _ssd_out_kernel(h_ref, yf_ref, yb_ref, z_ref, mod_ref, g_ref, ng_ref, wo_ref, o_ref, v_scr):
    gw = SSD_INNER // SSD_GROUPS
    for g in range(SSD_GROUPS):
        cols = slice(g * gw, (g + 1) * gw)
        y = (yf_ref[0, :, cols].astype(F32) + yb_ref[0, :, cols].astype(F32)) \
            * _silu(z_ref[0, :, cols].astype(F32))
        y = y * lax.rsqrt(jnp.mean(y * y, axis=-1, keepdims=True) + EPS)
        v_scr[:, cols] = (y * ng_ref[:, cols]).astype(BF16)
    out = jnp.dot(v_scr[...], wo_ref[...], preferred_element_type=F32)
    o_ref[0] = h_ref[0] + mod_ref[0, 5:6, :] * _rms(out, g_ref[3:4, :])


def _ssd_out(h, yf, yb, z, mod, g, norm_g, w_out, *, tm, mod_row):
    bsz, l, _ = h.shape
    tok = lambda w: pl.BlockSpec((1, tm, w), lambda b, t: (b, t, 0))
    return pl.pallas_call(
        _ssd_out_kernel,
        grid=(bsz, l // tm),
        in_specs=[
            tok(D_MODEL), tok(SSD_INNER), tok(SSD_INNER), tok(SSD_INNER),
            pl.BlockSpec((1, N_MOD, D_MODEL), lambda b, t: (mod_row(b), 0, 0)),
            _const_spec((6, D_MODEL)),
            _const_spec((1, SSD_INNER)),
            _const_spec((SSD_INNER, D_MODEL)),
        ],
        out_specs=tok(D_MODEL),
        out_shape=jax.ShapeDtypeStruct(h.shape, F32),
        scratch_shapes=[pltpu.VMEM((tm, SSD_INNER), BF16)],
        compiler_params=_cparams("parallel", "parallel"),
        name="ssd_out",
    )(h, yf, yb, z, mod, g, norm_g.reshape(1, -1), w_out)


def _gmlp_kernel(h_ref, mod_ref, g_ref, wu_ref, wv_ref, vg_ref, vb_ref, ws_ref, bs_ref, wo_ref, o_ref,
                 p_scr, *, tm):
    h = h_ref[0]
    u = (_rms(h, g_ref[2:3, :]) * (1.0 + mod_ref[0, 4:5, :]) + mod_ref[0, 3:4, :]).astype(BF16)
    gv = jax.nn.gelu(jnp.dot(u, wv_ref[...], preferred_element_type=F32), approximate=True)
    mu = jnp.mean(gv, axis=-1, keepdims=True)
    var = jnp.mean(jnp.square(gv - mu), axis=-1, keepdims=True)
    gvn = ((gv - mu) * lax.rsqrt(var + EPS) * vg_ref[...] + vb_ref[...]).astype(BF16)
    for g in range(GM_GROUPS):
        cols = slice(g * GM_GROUP_DIM, (g + 1) * GM_GROUP_DIM)
        gu = jax.nn.gelu(jnp.dot(u, wu_ref[:, cols], preferred_element_type=F32), approximate=True)
        for c in range(tm // GM_CHUNK):
            rows = slice(c * GM_CHUNK, (c + 1) * GM_CHUNK)
            s = jnp.dot(ws_ref[g], gvn[rows, cols], preferred_element_type=F32) + bs_ref[:, cols]
            p_scr[rows, cols] = (gu[rows, :] * s).astype(BF16)
    y = jnp.dot(p_scr[...], wo_ref[...], preferred_element_type=F32)
    o_ref[0] = h + mod_ref[0, 5:6, :] * _rms(y, g_ref[3:4, :])


def _gmlp(h, mod, g, w_in, v_g, v_b, w_s, b_s_full, w_out, *, tm, mod_row):
    bsz, l, _ = h.shape
    return pl.pallas_call(
        functools.partial(_gmlp_kernel, tm=tm),
        grid=(bsz, l // tm),
        in_specs=[
            pl.BlockSpec((1, tm, D_MODEL), lambda b, t: (b, t, 0)),
            pl.BlockSpec((1, N_MOD, D_MODEL), lambda b, t: (mod_row(b), 0, 0)),
            _const_spec((6, D_MODEL)),
            pl.BlockSpec((D_MODEL, GM_INNER), lambda b, t: (0, 0)),
            pl.BlockSpec((D_MODEL, GM_INNER), lambda b, t: (0, 1)),
            _const_spec((1, GM_INNER)),
            _const_spec((1, GM_INNER)),
            _const_spec((GM_GROUPS, GM_CHUNK, GM_CHUNK)),
            _const_spec((GM_CHUNK, GM_INNER)),
            _const_spec((GM_INNER, D_MODEL)),
        ],
        out_specs=pl.BlockSpec((1, tm, D_MODEL), lambda b, t: (b, t, 0)),
        out_shape=jax.ShapeDtypeStruct(h.shape, F32),
        scratch_shapes=[pltpu.VMEM((tm, GM_INNER), BF16)],
        compiler_params=_cparams("parallel", "parallel"),
        name="gmlp",
    )(h, mod, g, w_in, w_in, v_g.reshape(1, -1), v_b.reshape(1, -1), w_s, b_s_full, w_out)


def kernel(x, c, ctx, c_ctx, ada_w, ada_b, norm_g, ffn_w_in, ffn_w_out, ssd_w_in, ssd_conv_w, ssd_conv_b,
           ssd_dt_bias, ssd_A_log, ssd_D, ssd_norm_g, ssd_w_out, gm_w_in, gm_v_g, gm_v_b, gm_w_s, gm_b_s,
           gm_w_out):
    bsz, seq, _ = x.shape
    ctx_len = ctx.shape[1]
    cvec = jnp.zeros((MOD_ROWS, D_MODEL), F32).at[:bsz].set(c).at[bsz].set(c_ctx)
    mod_all = _ada_mod(cvec, ada_w, ada_b).reshape(DEPTH, MOD_ROWS, N_MOD, D_MODEL)

    x_row = lambda b: b
    ctx_row = lambda b: bsz
    tm_x = 512
    tm_c = ctx_len

    for i in range(DEPTH):
        use_ssd = (i % 2) == 0
        j = i // 2
        last = i == DEPTH - 1
        ctx_needed = (not last) or use_ssd
        ctx_full = not last
        mod = mod_all[i]
        g = norm_g[i]
        w1_in, w1_out = ffn_w_in[i, 0].astype(BF16), ffn_w_out[i, 0].astype(BF16)
        w2_in, w2_out = ffn_w_in[i, 1].astype(BF16), ffn_w_out[i, 1].astype(BF16)

        x = _ffn(x, mod, g, w1_in, w1_out, s=0, tm=tm_x, mod_row=x_row)
        if ctx_needed:
            ctx = _ffn(ctx, mod, g, w1_in, w1_out, s=0, tm=tm_c, mod_row=ctx_row)

        if use_ssd:
            w_in = ssd_w_in[j].astype(BF16)
            wz = w_in[:, :SSD_INNER]
            wx = w_in[:, SSD_INNER:SSD_INNER + SSD_CONV_DIM]
            wdt = w_in[:, SSD_INNER + SSD_CONV_DIM:]
            w_out = ssd_w_out[j].astype(BF16)
            d_skip = jnp.repeat(ssd_D[j], SSD_HEAD_DIM).reshape(1, SSD_INNER)
            dtb = ssd_dt_bias[j].reshape(-1)

            def branch(h, h0, tm, mod_row):
                z, xbc, dt = _ssd_in(h, mod, g, wz, wx, wdt, ssd_conv_w[j], ssd_conv_b[j], dtb,
                                     tm=tm, mod_row=mod_row)
                yf, yb, hT = _ssd_scan(xbc, dt, ssd_A_log[j], d_skip, h0)
                return yf, yb, z, hT

            h0 = jnp.zeros((bsz, 2, SSD_INNER, SSD_STATE), F32)
            cyf, cyb, cz, h_ctx = branch(ctx, h0, tm_c, ctx_row)
            xyf, xyb, xz, _ = branch(x, h_ctx, tm_x, x_row)
            x = _ssd_out(x, xyf, xyb, xz, mod, g, ssd_norm_g[j], w_out, tm=tm_x, mod_row=x_row)
            if ctx_full:
                ctx = _ssd_out(ctx, cyf, cyb, cz, mod, g, ssd_norm_g[j], w_out, tm=tm_c, mod_row=ctx_row)
        else:
            gw_in = gm_w_in[j].astype(BF16)
            gw_out = gm_w_out[j].astype(BF16)
            ws = gm_w_s[j].astype(BF16)
            bs_full = jnp.repeat(gm_b_s[j].T, GM_GROUP_DIM, axis=1)
            x = _gmlp(x, mod, g, gw_in, gm_v_g[j], gm_v_b[j], ws, bs_full, gw_out, tm=tm_x, mod_row=x_row)
            if ctx_full:
                ctx = _gmlp(ctx, mod, g, gw_in, gm_v_g[j], gm_v_b[j], ws, bs_full, gw_out,
                            tm=tm_c, mod_row=ctx_row)

        x = _ffn(x, mod, g, w2_in, w2_out, s=2, tm=tm_x, mod_row=x_row)
        if ctx_full:
            ctx = _ffn(ctx, mod, g, w2_in, w2_out, s=2, tm=tm_c, mod_row=ctx_row)
    return x
```

```python
import functools

import jax
import jax.numpy as jnp
from jax import lax
from jax.experimental import pallas as pl
from jax.experimental.pallas import tpu as pltpu

F32 = jnp.float32
BF16 = jnp.bfloat16

D_MODEL = 1024
DEPTH = 2
N_MOD = 9
MACARON_W = 0.5
EPS = 1e-6
FFN_DIM = 2816

SSD_INNER = 2048
SSD_HEAD_DIM = 64
SSD_HEADS = 32
SSD_GROUPS = 8
SSD_HPG = 4
SSD_STATE = 128
SSD_CONV = 5
SSD_CHUNK = 128
SSD_GN = SSD_GROUPS * SSD_STATE
SSD_CONV_DIM = SSD_INNER + 2 * SSD_GN

GM_CHUNK = 128
GM_INNER = 2048
GM_GROUPS = 8
GM_GROUP_DIM = 256

MXU_TILE = 256
LANES = 128
SUBLANES = 8
HALO = 16
VMEM_LIMIT = 56 * 1024 * 1024
MOD_ROWS = 16


def _cparams(*sem):
    return pltpu.CompilerParams(dimension_semantics=sem, vmem_limit_bytes=VMEM_LIMIT)


def _rms(x, g):
    return x * lax.rsqrt(jnp.mean(x * x, axis=-1, keepdims=True) + EPS) * g


def _silu(x):
    return x * jax.nn.sigmoid(x)


def _const_spec(shape):
    nd = len(shape)
    return pl.BlockSpec(shape, lambda *_: (0,) * nd)


def _ada_kernel(c_ref, w_ref, b_ref, o_ref):
    cs = _silu(c_ref[...]).astype(BF16)
    o_ref[...] = jnp.dot(cs, w_ref[...].astype(BF16), preferred_element_type=F32) + b_ref[...]


def _ada_mod(cvec, ada_w, ada_b):
    nd = N_MOD * D_MODEL
    bn = 1024
    return pl.pallas_call(
        _ada_kernel,
        grid=(DEPTH, nd // bn),
        in_specs=[
            pl.BlockSpec((MOD_ROWS, D_MODEL), lambda i, j: (0, 0)),
            pl.BlockSpec((None, D_MODEL, bn), lambda i, j: (i, 0, j)),
            pl.BlockSpec((None, 1, bn), lambda i, j: (i, 0, j)),
        ],
        out_specs=pl.BlockSpec((None, MOD_ROWS, bn), lambda i, j: (i, 0, j)),
        out_shape=jax.ShapeDtypeStruct((DEPTH, MOD_ROWS, nd), F32),
        compiler_params=_cparams("parallel", "parallel"),
        name="ada_mod",
    )(cvec, ada_w, ada_b.reshape(DEPTH, 1, nd))


def _ffn_kernel(h_ref, mod_ref, g_ref, wg_ref, wu_ref, wo_ref, o_ref, a_scr, *, s):
    h = h_ref[0]
    shift = mod_ref[0, 3 * s:3 * s + 1, :]
    scale = mod_ref[0, 3 * s + 1:3 * s + 2, :]
    gate = mod_ref[0, 3 * s + 2:3 * s + 3, :]
    u = (_rms(h, g_ref[2 * s:2 * s + 1, :]) * (1.0 + scale) + shift).astype(BF16)
    for c in range(FFN_DIM // MXU_TILE):
        cols = slice(c * MXU_TILE, (c + 1) * MXU_TILE)
        hg = jnp.dot(u, wg_ref[:, cols], preferred_element_type=F32)
        hu = jnp.dot(u, wu_ref[:, cols], preferred_element_type=F32)
        a_scr[:, cols] = (_silu(hg) * hu).astype(BF16)
    y = jnp.dot(a_scr[...], wo_ref[...], preferred_element_type=F32)
    o_ref[0] = h + (MACARON_W * gate) * _rms(y, g_ref[2 * s + 1:2 * s + 2, :])


def _ffn(h, mod, g, w_in, w_out, *, s, tm, mod_row):
    bsz, l, _ = h.shape
    return pl.pallas_call(
        functools.partial(_ffn_kernel, s=s),
        grid=(bsz, l // tm),
        in_specs=[
            pl.BlockSpec((1, tm, D_MODEL), lambda b, t: (b, t, 0)),
            pl.BlockSpec((1, N_MOD, D_MODEL), lambda b, t: (mod_row(b), 0, 0)),
            _const_spec((6, D_MODEL)),
            pl.BlockSpec((D_MODEL, FFN_DIM), lambda b, t: (0, 0)),
            pl.BlockSpec((D_MODEL, FFN_DIM), lambda b, t: (0, 1)),
            _const_spec((FFN_DIM, D_MODEL)),
        ],
        out_specs=pl.BlockSpec((1, tm, D_MODEL), lambda b, t: (b, t, 0)),
        out_shape=jax.ShapeDtypeStruct(h.shape, F32),
        scratch_shapes=[pltpu.VMEM((tm, FFN_DIM), BF16)],
        compiler_params=_cparams("parallel", "parallel"),
        name="ffn",
    )(h, mod, g, w_in, w_in, w_out)


def _ssd_in_kernel(xm_ref, xp_ref, xn_ref, mod_ref, g_ref, wz_ref, wx_ref, wdt_ref, cw_ref, cb_ref,
                   dtb_ref, z_ref, xc_ref, bt_ref, dt_ref, u_scr, p_scr, o_scr, *, tm):
    t = pl.program_id(1)
    nt = pl.num_programs(1)
    shift = mod_ref[0, 3:4, :]
    scale = mod_ref[0, 4:5, :]
    g_pre = g_ref[2:3, :]

    def prep(h):
        return _rms(h, g_pre) * (1.0 + scale) + shift

    u_scr[0:HALO, :] = jnp.where(t == 0, 0.0, prep(xp_ref[0])).astype(BF16)
    u_scr[HALO:HALO + tm, :] = prep(xm_ref[0]).astype(BF16)
    u_scr[HALO + tm:, :] = jnp.where(t == nt - 1, 0.0, prep(xn_ref[0])).astype(BF16)
    um = u_scr[HALO:HALO + tm, :]
    dt_raw = jnp.dot(um, wdt_ref[...], preferred_element_type=F32) + dtb_ref[...]
    dt_ref[0] = jnp.maximum(dt_raw, 0.0) + jnp.log1p(jnp.exp(-jnp.abs(dt_raw)))

    nblk = SSD_CONV_DIM // LANES
    npair = nblk // 2
    rows8 = tm // SUBLANES

    def project(pair):
        cols = slice(pair * MXU_TILE, (pair + 1) * MXU_TILE)
        p = jnp.dot(u_scr[...], wx_ref[:, cols], preferred_element_type=F32)
        p_scr[2 * pair] = p[:, :LANES]
        p_scr[2 * pair + 1] = p[:, LANES:]

    def conv_block(blk):
        cols = slice(blk * LANES, (blk + 1) * LANES)
        wk = [jnp.broadcast_to(cw_ref[k:k + 1, cols], (SUBLANES, LANES)) for k in range(SSD_CONV)]
        bias = jnp.broadcast_to(cb_ref[:, cols], (SUBLANES, LANES))
        slot = blk % 2
        for s in range(SUBLANES):
            acc = bias[None]
            for k in range(SSD_CONV):
                r0 = HALO - SSD_CONV // 2 + s + k
                tap = p_scr[blk, pl.ds(r0, rows8, stride=SUBLANES), :]
                acc = acc + wk[k][None] * tap.reshape(rows8 // SUBLANES, SUBLANES, LANES)
            o_scr[slot, pl.ds(s, rows8, stride=SUBLANES), :] = _silu(acc).reshape(rows8, LANES)
        val = o_scr[slot]
        nx = SSD_INNER // LANES
        nb = SSD_GN // LANES
        if blk < nx:
            xc_ref[0, :, cols] = val.astype(BF16)
        elif blk < nx + nb:
            rows = slice((blk - nx) * LANES, (blk - nx + 1) * LANES)
            bt_ref[0, rows, :] = val.T.astype(BF16)
        else:
            oc = slice((blk - nb) * LANES, (blk - nb + 1) * LANES)
            xc_ref[0, :, oc] = val.astype(BF16)

    project(0)
    z_ref[0] = jnp.dot(um, wz_ref[...], preferred_element_type=F32).astype(BF16)
    for pair in range(npair):
        if pair + 1 < npair:
            project(pair + 1)
        conv_block(2 * pair)
        conv_block(2 * pair + 1)


def _ssd_in(h, mod, g, wz, wx, wdt, conv_w, conv_b, dt_bias, *, tm, mod_row):
    bsz, l, _ = h.shape
    nh = tm // HALO
    last = l // HALO - 1
    return pl.pallas_call(
        functools.partial(_ssd_in_kernel, tm=tm),
        grid=(bsz, l // tm),
        in_specs=[
            pl.BlockSpec((1, tm, D_MODEL), lambda b, t: (b, t, 0)),
            pl.BlockSpec((1, HALO, D_MODEL), lambda b, t: (b, jnp.maximum(t * nh - 1, 0), 0)),
            pl.BlockSpec((1, HALO, D_MODEL), lambda b, t: (b, jnp.minimum((t + 1) * nh, last), 0)),
            pl.BlockSpec((1, N_MOD, D_MODEL), lambda b, t: (mod_row(b), 0, 0)),
            _const_spec((6, D_MODEL)),
            _const_spec((D_MODEL, SSD_INNER)),
            _const_spec((D_MODEL, SSD_CONV_DIM)),
            _const_spec((D_MODEL, 2 * SSD_HEADS)),
            _const_spec((SSD_CONV, SSD_CONV_DIM)),
            _const_spec((1, SSD_CONV_DIM)),
            _const_spec((1, 2 * SSD_HEADS)),
        ],
        out_specs=[
            pl.BlockSpec((1, tm, SSD_INNER), lambda b, t: (b, t, 0)),
            pl.BlockSpec((1, tm, SSD_INNER + SSD_GN), lambda b, t: (b, t, 0)),
            pl.BlockSpec((1, SSD_GN, tm), lambda b, t: (b, 0, t)),
            pl.BlockSpec((1, tm, 2 * SSD_HEADS), lambda b, t: (b, t, 0)),
        ],
        out_shape=[
            jax.ShapeDtypeStruct((bsz, l, SSD_INNER), BF16),
            jax.ShapeDtypeStruct((bsz, l, SSD_INNER + SSD_GN), BF16),
            jax.ShapeDtypeStruct((bsz, SSD_GN, l), BF16),
            jax.ShapeDtypeStruct((bsz, l, 2 * SSD_HEADS), F32),
        ],
        scratch_shapes=[
            pltpu.VMEM((tm + 2 * HALO, D_MODEL), BF16),
            pltpu.VMEM((SSD_CONV_DIM // LANES, tm + 2 * HALO, LANES), F32),
            pltpu.VMEM((2, tm, LANES), F32),
        ],
        compiler_params=_cparams("parallel", "parallel"),
        name="ssd_in",
    )(h, h, h, mod, g, wz, wx, wdt, conv_w, conv_b.reshape(1, -1), dt_bias.reshape(1, -1))


def _split_bf16(v, n):
    parts = []
    for _ in range(n):
        p = v.astype(BF16).astype(F32)
        parts.append(p)
        v = v - p
    return parts


def _lane_replication_tables():
    r = jnp.arange(LANES)[:, None]
    ca = jnp.arange(SSD_HEADS * SSD_CHUNK)[None, :]
    ind_a = ((r % SSD_HEADS) == (ca // SSD_CHUNK)) & (r < 3 * SSD_HEADS)
    ce = jnp.arange(2 * SSD_INNER)[None, :]
    ind_ew = ((r % SSD_HEADS) == ((ce % SSD_INNER) // SSD_HEAD_DIM)) \
        & ((r // (2 * SSD_HEADS)) == (ce // SSD_INNER))
    return ind_a.astype(BF16), ind_ew.astype(BF16)


def _scan_dir(d, xc_ref, bt_ref, dt_ref, alog_ref, dskip_ref, inda_ref, indew_ref, y_ref, h_ref):
    q = SSD_CHUNK
    nh = SSD_HEADS
    dt = dt_ref[0][:, nh * d:nh * (d + 1)]
    a_neg = -jnp.exp(alog_ref[d:d + 1, :])
    row = lax.broadcasted_iota(jnp.int32, (q, q), 0)
    col = lax.broadcasted_iota(jnp.int32, (q, q), 1)
    tri = (row >= col) if d == 0 else (row <= col)
    pad_a = jnp.zeros((q, LANES - 3 * nh), F32)
    cs = jnp.dot(tri.astype(BF16),
                 jnp.concatenate(_split_bf16(dt * a_neg, 3) + [pad_a], axis=1).astype(BF16),
                 preferred_element_type=F32)
    a = cs[:, 0:nh] + cs[:, nh:2 * nh] + cs[:, 2 * nh:3 * nh]
    a_end = a[q - 1:q, :] if d == 0 else a[0:1, :]
    e_in = jnp.exp(a)
    w_end = dt * jnp.exp(a_end - a)
    st_t = jnp.concatenate([a, dt, jnp.zeros((q, LANES - 2 * nh), F32)], axis=1).T
    a_t = st_t[0:nh]
    dt_t = st_t[nh:2 * nh]
    a_parts = jnp.concatenate(_split_bf16(a, 3) + [pad_a], axis=1).astype(BF16)
    ew_parts = jnp.concatenate(_split_bf16(e_in, 2) + _split_bf16(w_end, 2), axis=1).astype(BF16)
    low_half = lax.broadcasted_iota(jnp.int32, (q, 2 * SSD_HEAD_DIM), 1) < SSD_HEAD_DIM

    gw = SSD_HPG * SSD_HEAD_DIM
    for g in range(SSD_GROUPS):
        gs = slice(gw * g, gw * (g + 1))
        c_g = xc_ref[0, :, SSD_INNER + SSD_STATE * g:SSD_INNER + SSD_STATE * (g + 1)]
        bt_g = bt_ref[0, SSD_STATE * g:SSD_STATE * (g + 1), :]
        x_g = xc_ref[0, :, gs]
        ht_g = h_ref[0, d, :, gs]
        cb = jnp.dot(c_g, bt_g, preferred_element_type=F32)
        a_rep = jnp.dot(a_parts, inda_ref[:, SSD_HPG * q * g:SSD_HPG * q * (g + 1)],
                        preferred_element_type=F32)
        e_rep = jnp.dot(ew_parts, indew_ref[:, gs], preferred_element_type=F32)
        w_rep = jnp.dot(ew_parts, indew_ref[:, SSD_INNER + gw * g:SSD_INNER + gw * (g + 1)],
                        preferred_element_type=F32)
        y_g = jnp.dot(c_g, ht_g.astype(BF16), preferred_element_type=F32) * e_rep
        if d == 0:
            y_g = y_g + dskip_ref[:, gs] * x_g.astype(F32)
        ms = []
        for k in range(SSD_HPG):
            hd = SSD_HPG * g + k
            seg = a_rep[:, q * k:q * (k + 1)] - a_t[hd:hd + 1, :]
            decay = jnp.exp(jnp.where(tri, seg, -jnp.inf))
            ms.append((cb * decay * dt_t[hd:hd + 1, :]).astype(BF16))
        y_pairs = []
        for pr in range(SSD_HPG // 2):
            x_pair = x_g[:, 2 * SSD_HEAD_DIM * pr:2 * SSD_HEAD_DIM * (pr + 1)]
            zero = jnp.zeros_like(x_pair)
            x_bd = jnp.concatenate([jnp.where(low_half, x_pair, zero),
                                    jnp.where(low_half, zero, x_pair)], axis=0)
            m_pair = jnp.concatenate([ms[2 * pr], ms[2 * pr + 1]], axis=1)
            y_pairs.append(jnp.dot(m_pair, x_bd, preferred_element_type=F32))
        y_ref[0, :, gs] = (y_g + jnp.concatenate(y_pairs, axis=1)).astype(BF16)
        xw = (x_g.astype(F32) * w_rep).astype(BF16)
        s_g = jnp.dot(bt_g, xw, preferred_element_type=F32)
        dec_row = e_rep[q - 1:q, :] if d == 0 else e_rep[0:1, :]
        h_ref[0, d, :, gs] = ht_g * dec_row + s_g


def _ssd_scan_kernel(xf_ref, xb_ref, btf_ref, btb_ref, dtf_ref, dtb_ref, alog_ref, dskip_ref,
                     inda_ref, indew_ref, h0_ref, yf_ref, yb_ref, h_ref):
    @pl.when(pl.program_id(1) == 0)
    def _():
        h_ref[...] = h0_ref[...]

    _scan_dir(0, xf_ref, btf_ref, dtf_ref, alog_ref, dskip_ref, inda_ref, indew_ref, yf_ref, h_ref)
    _scan_dir(1, xb_ref, btb_ref, dtb_ref, alog_ref, dskip_ref, inda_ref, indew_ref, yb_ref, h_ref)


def _ssd_scan(xc, bt, dt, a_log, d_skip, h0):
    bsz, l, _ = xc.shape
    nc = l // SSD_CHUNK
    fwd = lambda b, c: (b, c, 0)
    bwd = lambda b, c: (b, nc - 1 - c, 0)
    fwd_t = lambda b, c: (b, 0, c)
    bwd_t = lambda b, c: (b, 0, nc - 1 - c)
    st_spec = pl.BlockSpec((1, 2, SSD_STATE, SSD_INNER), lambda b, c: (b, 0, 0, 0))
    ind_a, ind_ew = _lane_replication_tables()
    return pl.pallas_call(
        _ssd_scan_kernel,
        grid=(bsz, nc),
        in_specs=[
            pl.BlockSpec((1, SSD_CHUNK, SSD_INNER + SSD_GN), fwd),
            pl.BlockSpec((1, SSD_CHUNK, SSD_INNER + SSD_GN), bwd),
            pl.BlockSpec((1, SSD_GN, SSD_CHUNK), fwd_t),
            pl.BlockSpec((1, SSD_GN, SSD_CHUNK), bwd_t),
            pl.BlockSpec((1, SSD_CHUNK, 2 * SSD_HEADS), fwd),
            pl.BlockSpec((1, SSD_CHUNK, 2 * SSD_HEADS), bwd),
            _const_spec((2, SSD_HEADS)),
            _const_spec((1, SSD_INNER)),
            _const_spec(ind_a.shape),
            _const_spec(ind_ew.shape),
            st_spec,
        ],
        out_specs=[
            pl.BlockSpec((1, SSD_CHUNK, SSD_INNER), fwd),
            pl.BlockSpec((1, SSD_CHUNK, SSD_INNER), bwd),
            st_spec,
        ],
        out_shape=[
            jax.ShapeDtypeStruct((bsz, l, SSD_INNER), BF16),
            jax.ShapeDtypeStruct((bsz, l, SSD_INNER), BF16),
            jax.ShapeDtypeStruct(h0.shape, F32),
        ],
        compiler_params=_cparams("parallel", "arbitrary"),
        name="ssd_scan",
    )(xc, xc, bt, bt, dt, dt, a_log, d_skip, ind_a, ind_ew, h0)


def _ssd_out_kernel(h_ref, yf_ref, yb_ref, z_ref, mod_ref, g_ref, ng_ref, wo_ref, o_ref, v_scr):
    gw = SSD_INNER // SSD_GROUPS
    for g in range(SSD_GROUPS):
        cols = slice(g * gw, (g + 1) * gw)
        y = (yf_ref[0, :, cols].astype(F32) + yb_ref[0, :, cols].astype(F32)) \
            * _silu(z_ref[0, :, cols].astype(F32))
        y = y * lax.rsqrt(jnp.mean(y * y, axis=-1, keepdims=True) + EPS)
        v_scr[:, cols] = (y * ng_ref[:, cols]).astype(BF16)
    out = jnp.dot(v_scr[...], wo_ref[...], preferred_element_type=F32)
    o_ref[0] = h_ref[0] + mod_ref[0, 5:6, :] * _rms(out, g_ref[3:4, :])


def _ssd_out(h, yf, yb, z, mod, g, norm_g, w_out, *, tm, mod_row):
    bsz, l, _ = h.shape
    tok = lambda w: pl.BlockSpec((1, tm, w), lambda b, t: (b, t, 0))
    return pl.pallas_call(
        _ssd_out_kernel,
        grid=(bsz, l // tm),
        in_specs=[
            tok(D_MODEL), tok(SSD_INNER), tok(SSD_INNER), tok(SSD_INNER),
            pl.BlockSpec((1, N_MOD, D_MODEL), lambda b, t: (mod_row(b), 0, 0)),
            _const_spec((6, D_MODEL)),
            _const_spec((1, SSD_INNER)),
            _const_spec((SSD_INNER, D_MODEL)),
        ],
        out_specs=tok(D_MODEL),
        out_shape=jax.ShapeDtypeStruct(h.shape, F32),
        scratch_shapes=[pltpu.VMEM((tm, SSD_INNER), BF16)],
        compiler_params=_cparams("parallel", "parallel"),
        name="ssd_out",
    )(h, yf, yb, z, mod, g, norm_g.reshape(1, -1), w_out)


def _gmlp_kernel(h_ref, mod_ref, g_ref, wu_ref, wv_ref, vg_ref, vb_ref, ws_ref, bs_ref, wo_ref, o_ref,
                 p_scr, *, tm):
    h = h_ref[0]
    u = (_rms(h, g_ref[2:3, :]) * (1.0 + mod_ref[0, 4:5, :]) + mod_ref[0, 3:4, :]).astype(BF16)
    gv = jax.nn.gelu(jnp.dot(u, wv_ref[...], preferred_element_type=F32), approximate=True)
    mu = jnp.mean(gv, axis=-1, keepdims=True)
    var = jnp.mean(jnp.square(gv - mu), axis=-1, keepdims=True)
    gvn = ((gv - mu) * lax.rsqrt(var + EPS) * vg_ref[...] + vb_ref[...]).astype(BF16)
    for g in range(GM_GROUPS):
        cols = slice(g * GM_GROUP_DIM, (g + 1) * GM_GROUP_DIM)
        gu = jax.nn.gelu(jnp.dot(u, wu_ref[:, cols], preferred_element_type=F32), approximate=True)
        for c in range(tm // GM_CHUNK):
            rows = slice(c * GM_CHUNK, (c + 1) * GM_CHUNK)
            s = jnp.dot(ws_ref[g], gvn[rows, cols], preferred_element_type=F32) + bs_ref[:, cols]
            p_scr[rows, cols] = (gu[rows, :] * s).astype(BF16)
    y = jnp.dot(p_scr[...], wo_ref[...], preferred_element_type=F32)
    o_ref[0] = h + mod_ref[0, 5:6, :] * _rms(y, g_ref[3:4, :])


def _gmlp(h, mod, g, w_in, v_g, v_b, w_s, b_s_full, w_out, *, tm, mod_row):
    bsz, l, _ = h.shape
    return pl.pallas_call(
        functools.partial(_gmlp_kernel, tm=tm),
        grid=(bsz, l // tm),
        in_specs=[
            pl.BlockSpec((1, tm, D_MODEL), lambda b, t: (b, t, 0)),
            pl.BlockSpec((1, N_MOD, D_MODEL), lambda b, t: (mod_row(b), 0, 0)),
            _const_spec((6, D_MODEL)),
            pl.BlockSpec((D_MODEL, GM_INNER), lambda b, t: (0, 0)),
            pl.BlockSpec((D_MODEL, GM_INNER), lambda b, t: (0, 1)),
            _const_spec((1, GM_INNER)),
            _const_spec((1, GM_INNER)),
            _const_spec((GM_GROUPS, GM_CHUNK, GM_CHUNK)),
            _const_spec((GM_CHUNK, GM_INNER)),
            _const_spec((GM_INNER, D_MODEL)),
        ],
        out_specs=pl.BlockSpec((1, tm, D_MODEL), lambda b, t: (b, t, 0)),
        out_shape=jax.ShapeDtypeStruct(h.shape, F32),
        scratch_shapes=[pltpu.VMEM((tm, GM_INNER), BF16)],
        compiler_params=_cparams("parallel", "parallel"),
        name="gmlp",
    )(h, mod, g, w_in, w_in, v_g.reshape(1, -1), v_b.reshape(1, -1), w_s, b_s_full, w_out)


def kernel(x, c, ctx, c_ctx, ada_w, ada_b, norm_g, ffn_w_in, ffn_w_out, ssd_w_in, ssd_conv_w, ssd_conv_b,
           ssd_dt_bias, ssd_A_log, ssd_D, ssd_norm_g, ssd_w_out, gm_w_in, gm_v_g, gm_v_b, gm_w_s, gm_b_s,
           gm_w_out):
    bsz, seq, _ = x.shape
    ctx_len = ctx.shape[1]
    cvec = jnp.zeros((MOD_ROWS, D_MODEL), F32).at[:bsz].set(c).at[bsz].set(c_ctx)
    mod_all = _ada_mod(cvec, ada_w, ada_b).reshape(DEPTH, MOD_ROWS, N_MOD, D_MODEL)

    x_row = lambda b: b
    ctx_row = lambda b: bsz
    tm_x = 512
    tm_c = ctx_len

    for i in range(DEPTH):
        use_ssd = (i % 2) == 0
        j = i // 2
        last = i == DEPTH - 1
        ctx_needed = (not last) or use_ssd
        ctx_full = not last
        mod = mod_all[i]
        g = norm_g[i]
        w1_in, w1_out = ffn_w_in[i, 0].astype(BF16), ffn_w_out[i, 0].astype(BF16)
        w2_in, w2_out = ffn_w_in[i, 1].astype(BF16), ffn_w_out[i, 1].astype(BF16)

        x = _ffn(x, mod, g, w1_in, w1_out, s=0, tm=tm_x, mod_row=x_row)
        if ctx_needed:
            ctx = _ffn(ctx, mod, g, w1_in, w1_out, s=0, tm=tm_c, mod_row=ctx_row)

        if use_ssd:
            w_in = ssd_w_in[j].astype(BF16)
            wz = w_in[:, :SSD_INNER]
            wx = w_in[:, SSD_INNER:SSD_INNER + SSD_CONV_DIM]
            wdt = w_in[:, SSD_INNER + SSD_CONV_DIM:]
            w_out = ssd_w_out[j].astype(BF16)
            d_skip = jnp.repeat(ssd_D[j], SSD_HEAD_DIM).reshape(1, SSD_INNER)
            dtb = ssd_dt_bias[j].reshape(-1)

            def branch(h, h0, tm, mod_row):
                z, xc, bt, dt = _ssd_in(h, mod, g, wz, wx, wdt, ssd_conv_w[j], ssd_conv_b[j], dtb,
                                        tm=tm, mod_row=mod_row)
                yf, yb, hT = _ssd_scan(xc, bt, dt, ssd_A_log[j], d_skip, h0)
                return yf, yb, z, hT

            h0 = jnp.zeros((bsz, 2, SSD_STATE, SSD_INNER), F32)
            cyf, cyb, cz, h_ctx = branch(ctx, h0, tm_c, ctx_row)
            xyf, xyb, xz, _ = branch(x, h_ctx, tm_x, x_row)
            x = _ssd_out(x, xyf, xyb, xz, mod, g, ssd_norm_g[j], w_out, tm=tm_x, mod_row=x_row)
            if ctx_full:
                ctx = _ssd_out(ctx, cyf, cyb, cz, mod, g, ssd_norm_g[j], w_out, tm=tm_c, mod_row=ctx_row)
        else:
            gw_in = gm_w_in[j].astype(BF16)
            gw_out = gm_w_out[j].astype(BF16)
            ws = gm_w_s[j].astype(BF16)
            bs_full = jnp.repeat(gm_b_s[j].T, GM_GROUP_DIM, axis=1)
            x = _gmlp(x, mod, g, gw_in, gm_v_g[j], gm_v_b[j], ws, bs_full, gw_out, tm=tm_x, mod_row=x_row)
            if ctx_full:
                ctx = _gmlp(ctx, mod, g, gw_in, gm_v_g[j], gm_v_b[j], ws, bs_full, gw_out,
                            tm=tm_c, mod_row=ctx_row)

        x = _ffn(x, mod, g, w2_in, w2_out, s=2, tm=tm_x, mod_row=x_row)
        if ctx_full:
            ctx = _ffn(ctx, mod, g, w2_in, w2_out, s=2, tm=tm_c, mod_row=ctx_row)
    return x
```

```python
import functools
import math

import jax
import jax.numpy as jnp
from jax import lax
from jax.experimental import pallas as pl
from jax.experimental.pallas import tpu as pltpu

F32 = jnp.float32
BF16 = jnp.bfloat16

D_MODEL = 1024
DEPTH = 2
N_MOD = 9
MACARON_W = 0.5
EPS = 1e-6
FFN_DIM = 2816

SSD_INNER = 2048
SSD_HEAD_DIM = 64
SSD_HEADS = 32
SSD_GROUPS = 8
SSD_HPG = 4
SSD_STATE = 128
SSD_CONV = 5
SSD_CHUNK = 128
SSD_GN = SSD_GROUPS * SSD_STATE
SSD_CONV_DIM = SSD_INNER + 2 * SSD_GN

GM_CHUNK = 128
GM_INNER = 2048
GM_GROUPS = 8
GM_GROUP_DIM = 256

MXU_TILE = 256
LANES = 128
SUBLANES = 8
BF16_ROWS = 16
HALO = 16
VMEM_LIMIT = 56 * 1024 * 1024
MOD_ROWS = 16
LOG2E = math.log2(math.e)
NEG_BIG = -1e30


def _cparams(*sem):
    return pltpu.CompilerParams(dimension_semantics=sem, vmem_limit_bytes=VMEM_LIMIT)


def _rms(x, g):
    return x * lax.rsqrt(jnp.mean(x * x, axis=-1, keepdims=True) + EPS) * g


def _silu(x):
    return x * jax.nn.sigmoid(x)


def _gelu_tanh(x):
    k = -2.0 * LOG2E * math.sqrt(2.0 / math.pi)
    t = x * (k + (k * 0.044715) * (x * x))
    return x / (1.0 + jnp.exp2(t))


def _const_spec(shape):
    nd = len(shape)
    return pl.BlockSpec(shape, lambda *_: (0,) * nd)


def _ada_kernel(c_ref, w_ref, b_ref, o_ref):
    cs = _silu(c_ref[...]).astype(BF16)
    o_ref[...] = jnp.dot(cs, w_ref[...].astype(BF16), preferred_element_type=F32) + b_ref[...]


def _ada_mod(cvec, ada_w, ada_b):
    nd = N_MOD * D_MODEL
    bn = 1024
    return pl.pallas_call(
        _ada_kernel,
        grid=(DEPTH, nd // bn),
        in_specs=[
            pl.BlockSpec((MOD_ROWS, D_MODEL), lambda i, j: (0, 0)),
            pl.BlockSpec((None, D_MODEL, bn), lambda i, j: (i, 0, j)),
            pl.BlockSpec((None, 1, bn), lambda i, j: (i, 0, j)),
        ],
        out_specs=pl.BlockSpec((None, MOD_ROWS, bn), lambda i, j: (i, 0, j)),
        out_shape=jax.ShapeDtypeStruct((DEPTH, MOD_ROWS, nd), F32),
        compiler_params=_cparams("parallel", "parallel"),
        name="ada_mod",
    )(cvec, ada_w, ada_b.reshape(DEPTH, 1, nd))


def _ffn_kernel(h_ref, mod_ref, g_ref, wg_ref, wu_ref, wo_ref, o_ref, a_scr, *, s, tm):
    shift = mod_ref[0, 3 * s:3 * s + 1, :]
    scale = mod_ref[0, 3 * s + 1:3 * s + 2, :]
    gate = mod_ref[0, 3 * s + 2:3 * s + 3, :]
    halves = [slice(r * (tm // 2), (r + 1) * (tm // 2)) for r in range(2)]
    us = [(_rms(h_ref[0, rows, :], g_ref[2 * s:2 * s + 1, :]) * (1.0 + scale) + shift).astype(BF16)
          for rows in halves]
    for c in range(FFN_DIM // MXU_TILE):
        cols = slice(c * MXU_TILE, (c + 1) * MXU_TILE)
        for rows, u in zip(halves, us):
            hg = jnp.dot(u, wg_ref[:, cols], preferred_element_type=F32)
            hu = jnp.dot(u, wu_ref[:, cols], preferred_element_type=F32)
            a_scr[rows, cols] = (_silu(hg) * hu).astype(BF16)
    for rows in halves:
        y = jnp.dot(a_scr[rows, :], wo_ref[...], preferred_element_type=F32)
        o_ref[0, rows, :] = h_ref[0, rows, :] + (MACARON_W * gate) * _rms(y, g_ref[2 * s + 1:2 * s + 2, :])


def _ffn(h, mod, g, w_in, w_out, *, s, tm, mod_row):
    bsz, l, _ = h.shape
    return pl.pallas_call(
        functools.partial(_ffn_kernel, s=s, tm=tm),
        grid=(bsz, l // tm),
        in_specs=[
            pl.BlockSpec((1, tm, D_MODEL), lambda b, t: (b, t, 0)),
            pl.BlockSpec((1, N_MOD, D_MODEL), lambda b, t: (mod_row(b), 0, 0)),
            _const_spec((6, D_MODEL)),
            pl.BlockSpec((D_MODEL, FFN_DIM), lambda b, t: (0, 0)),
            pl.BlockSpec((D_MODEL, FFN_DIM), lambda b, t: (0, 1)),
            _const_spec((FFN_DIM, D_MODEL)),
        ],
        out_specs=pl.BlockSpec((1, tm, D_MODEL), lambda b, t: (b, t, 0)),
        out_shape=jax.ShapeDtypeStruct(h.shape, F32),
        scratch_shapes=[pltpu.VMEM((tm, FFN_DIM), BF16)],
        compiler_params=_cparams("parallel", "parallel"),
        name="ffn",
    )(h, mod, g, w_in, w_in, w_out)


def _ssd_in_kernel(xm_ref, xp_ref, xn_ref, mod_ref, g_ref, wz_ref, wxa_ref, wxb_ref, wdt_ref, cw_ref, cb_ref,
                   dtb_ref, z_ref, xc_ref, bt_ref, dt_ref, u_scr, p_scr, o_scr, *, tm):
    t = pl.program_id(1)
    nt = pl.num_programs(1)
    shift = mod_ref[0, 3:4, :]
    scale = mod_ref[0, 4:5, :]
    g_pre = g_ref[2:3, :]

    def prep(h):
        return _rms(h, g_pre) * (1.0 + scale) + shift

    u_scr[0:HALO, :] = jnp.where(t == 0, 0.0, prep(xp_ref[0])).astype(BF16)
    u_scr[HALO:HALO + tm, :] = prep(xm_ref[0]).astype(BF16)
    u_scr[HALO + tm:, :] = jnp.where(t == nt - 1, 0.0, prep(xn_ref[0])).astype(BF16)
    um = u_scr[HALO:HALO + tm, :]
    dt_raw = jnp.dot(um, wdt_ref[...], preferred_element_type=F32) + dtb_ref[...]
    dt_ref[0] = jnp.maximum(dt_raw, 0.0) + jnp.log1p(jnp.exp(-jnp.abs(dt_raw)))

    nblk = SSD_CONV_DIM // LANES
    npair = nblk // 2
    pairs_a = SSD_INNER // MXU_TILE
    groups8 = tm // (SUBLANES * SUBLANES)

    def project(pair):
        if pair < pairs_a:
            w = wxa_ref[:, pair * MXU_TILE:(pair + 1) * MXU_TILE]
        else:
            w = wxb_ref[:, (pair - pairs_a) * MXU_TILE:(pair - pairs_a + 1) * MXU_TILE]
        p = jnp.dot(u_scr[...], w, preferred_element_type=F32)
        p_scr[2 * pair] = p[:, :LANES]
        p_scr[2 * pair + 1] = p[:, LANES:]

    def conv_block(blk):
        cols = slice(blk * LANES, (blk + 1) * LANES)
        wk = [jnp.broadcast_to(cw_ref[k:k + 1, cols], (SUBLANES, LANES)) for k in range(SSD_CONV)]
        bias = jnp.broadcast_to(cb_ref[:, cols], (SUBLANES, LANES))
        slot = blk % 2
        for m in range(groups8):
            base = HALO - SSD_CONV // 2 + SUBLANES * SUBLANES * m
            slabs = [p_scr[blk, pl.ds(base + r, SUBLANES, stride=SUBLANES), :]
                     for r in range(SUBLANES + SSD_CONV - 1)]
            for s in range(SUBLANES):
                acc = bias
                for k in range(SSD_CONV):
                    acc = acc + wk[k] * slabs[s + k]
                o_scr[slot, pl.ds(SUBLANES * SUBLANES * m + s, SUBLANES, stride=SUBLANES), :] = _silu(acc)
        val = o_scr[slot]
        nx = SSD_INNER // LANES
        nb = SSD_GN // LANES
        if blk < nx:
            xc_ref[0, :, cols] = val.astype(BF16)
        elif blk < nx + nb:
            rows = slice((blk - nx) * LANES, (blk - nx + 1) * LANES)
            bt_ref[0, rows, :] = val.T.astype(BF16)
        else:
            oc = slice((blk - nb) * LANES, (blk - nb + 1) * LANES)
            xc_ref[0, :, oc] = val.astype(BF16)

    project(0)
    z_ref[0] = jnp.dot(um, wz_ref[...], preferred_element_type=F32).astype(BF16)
    for pair in range(npair):
        conv_block(2 * pair)
        if pair + 1 < npair:
            project(pair + 1)
        conv_block(2 * pair + 1)


def _ssd_in(h, mod, g, w_in, conv_w, conv_b, dt_bias, *, tm, mod_row):
    bsz, l, _ = h.shape
    nh = tm // HALO
    last = l // HALO - 1
    wdt = w_in[:, SSD_INNER + SSD_CONV_DIM:]
    wblk = lambda j: pl.BlockSpec((D_MODEL, SSD_INNER), lambda b, t: (0, j))
    return pl.pallas_call(
        functools.partial(_ssd_in_kernel, tm=tm),
        grid=(bsz, l // tm),
        in_specs=[
            pl.BlockSpec((1, tm, D_MODEL), lambda b, t: (b, t, 0)),
            pl.BlockSpec((1, HALO, D_MODEL), lambda b, t: (b, jnp.maximum(t * nh - 1, 0), 0)),
            pl.BlockSpec((1, HALO, D_MODEL), lambda b, t: (b, jnp.minimum((t + 1) * nh, last), 0)),
            pl.BlockSpec((1, N_MOD, D_MODEL), lambda b, t: (mod_row(b), 0, 0)),
            _const_spec((6, D_MODEL)),
            wblk(0), wblk(1), wblk(2),
            _const_spec((D_MODEL, 2 * SSD_HEADS)),
            _const_spec((SSD_CONV, SSD_CONV_DIM)),
            _const_spec((1, SSD_CONV_DIM)),
            _const_spec((1, 2 * SSD_HEADS)),
        ],
        out_specs=[
            pl.BlockSpec((1, tm, SSD_INNER), lambda b, t: (b, t, 0)),
            pl.BlockSpec((1, tm, SSD_INNER + SSD_GN), lambda b, t: (b, t, 0)),
            pl.BlockSpec((1, SSD_GN, tm), lambda b, t: (b, 0, t)),
            pl.BlockSpec((1, tm, 2 * SSD_HEADS), lambda b, t: (b, t, 0)),
        ],
        out_shape=[
            jax.ShapeDtypeStruct((bsz, l, SSD_INNER), BF16),
            jax.ShapeDtypeStruct((bsz, l, SSD_INNER + SSD_GN), BF16),
            jax.ShapeDtypeStruct((bsz, SSD_GN, l), BF16),
            jax.ShapeDtypeStruct((bsz, l, 2 * SSD_HEADS), F32),
        ],
        scratch_shapes=[
            pltpu.VMEM((tm + 2 * HALO, D_MODEL), BF16),
            pltpu.VMEM((SSD_CONV_DIM // LANES, tm + 2 * HALO, LANES), F32),
            pltpu.VMEM((2, tm, LANES), F32),
        ],
        compiler_params=_cparams("parallel", "parallel"),
        name="ssd_in",
    )(h, h, h, mod, g, w_in, w_in, w_in, wdt, conv_w, conv_b.reshape(1, -1), dt_bias.reshape(1, -1))


def _split_bf16(v, n):
    parts = []
    for _ in range(n):
        p = v.astype(BF16).astype(F32)
        parts.append(p)
        v = v - p
    return parts


def _scan_tables():
    q = SSD_CHUNK
    r = jnp.arange(LANES)[:, None]
    ca = jnp.arange(SSD_HEADS * q)[None, :]
    ind_a = (((r % SSD_HEADS) == (ca // q)) & (r < 3 * SSD_HEADS)).astype(F32)
    j = ca % q
    mask_f = jnp.where(j > r, NEG_BIG, 0.0)
    mask_b = jnp.where(j < r, NEG_BIG, 0.0)
    seg_tab = jnp.stack([jnp.concatenate([ind_a, mask_f], axis=0),
                         jnp.concatenate([ind_a, mask_b], axis=0)]).astype(BF16)
    ce = jnp.arange(2 * SSD_INNER)[None, :]
    ind_ew = ((r % SSD_HEADS) == ((ce % SSD_INNER) // SSD_HEAD_DIM)) \
        & ((r // (2 * SSD_HEADS)) == (ce // SSD_INNER))
    return seg_tab, ind_ew.astype(BF16)


def _scan_dir(d, xc_ref, bt_ref, dt_ref, alog_ref, dskip_ref, tab_scr, indew_ref, y_ref, h_ref):
    q = SSD_CHUNK
    nh = SSD_HEADS
    dt = dt_ref[0][:, nh * d:nh * (d + 1)]
    a_neg = -jnp.exp(alog_ref[d:d + 1, :])
    row = lax.broadcasted_iota(jnp.int32, (q, q), 0)
    col = lax.broadcasted_iota(jnp.int32, (q, q), 1)
    tri = (row >= col) if d == 0 else (row <= col)
    pad_a = jnp.zeros((q, LANES - 3 * nh), F32)
    cs = jnp.dot(tri.astype(BF16),
                 jnp.concatenate(_split_bf16(dt * a_neg, 3) + [pad_a], axis=1).astype(BF16),
                 preferred_element_type=F32)
    a = cs[:, 0:nh] + cs[:, nh:2 * nh] + cs[:, 2 * nh:3 * nh]
    a_end = a[q - 1:q, :] if d == 0 else a[0:1, :]
    e_in = jnp.exp(a)
    w_end = dt * jnp.exp(a_end - a)
    r_nat = LOG2E * (jnp.maximum(jnp.log(dt), NEG_BIG) - a)
    r_t = jnp.concatenate([r_nat, jnp.zeros((q, LANES - nh), F32)], axis=1).T[0:nh]
    r_parts = _split_bf16(r_t, 3)
    prow = lax.broadcasted_iota(jnp.int32, (BF16_ROWS, q), 0)
    for hd in range(nh):
        tile = jnp.where(prow == 0, r_parts[0][hd:hd + 1, :],
                         jnp.where(prow == 1, r_parts[1][hd:hd + 1, :],
                                   jnp.where(prow == 2, r_parts[2][hd:hd + 1, :], 0.0)))
        tab_scr[d, 3 * nh:3 * nh + BF16_ROWS, q * hd:q * (hd + 1)] = tile.astype(BF16)
    ones3 = (lax.broadcasted_iota(jnp.int32, (q, LANES - 3 * nh), 1) < 3).astype(F32)
    eye = (row == col).astype(F32)
    seg_lhs = jnp.concatenate(_split_bf16(LOG2E * a, 3) + [ones3, eye], axis=1).astype(BF16)
    ew_parts = jnp.concatenate(_split_bf16(e_in, 2) + _split_bf16(w_end, 2), axis=1).astype(BF16)
    low_half = lax.broadcasted_iota(jnp.int32, (q, 2 * SSD_HEAD_DIM), 1) < SSD_HEAD_DIM

    gw = SSD_HPG * SSD_HEAD_DIM

    def group(g):
        gs = slice(gw * g, gw * (g + 1))
        c_g = xc_ref[0, :, SSD_INNER + SSD_STATE * g:SSD_INNER + SSD_STATE * (g + 1)]
        bt_g = bt_ref[0, SSD_STATE * g:SSD_STATE * (g + 1), :]
        x_g = xc_ref[0, :, gs]
        ht_g = h_ref[0, d, :, gs]
        cb = jnp.dot(c_g, bt_g, preferred_element_type=F32)
        seg = jnp.dot(seg_lhs, tab_scr[d, :, SSD_HPG * q * g:SSD_HPG * q * (g + 1)],
                      preferred_element_type=F32)
        e_rep = jnp.dot(ew_parts, indew_ref[:, gs], preferred_element_type=F32)
        w_rep = jnp.dot(ew_parts, indew_ref[:, SSD_INNER + gw * g:SSD_INNER + gw * (g + 1)],
                        preferred_element_type=F32)
        y_g = jnp.dot(c_g, ht_g.astype(BF16), preferred_element_type=F32) * e_rep
        if d == 0:
            y_g = y_g + dskip_ref[:, gs] * x_g.astype(F32)
        ms = [(cb * jnp.exp2(seg[:, q * k:q * (k + 1)])).astype(BF16) for k in range(SSD_HPG)]
        y_pairs = []
        for pr in range(SSD_HPG // 2):
            x_pair = x_g[:, 2 * SSD_HEAD_DIM * pr:2 * SSD_HEAD_DIM * (pr + 1)]
            zero = jnp.zeros_like(x_pair)
            x_bd = jnp.concatenate([jnp.where(low_half, x_pair, zero),
                                    jnp.where(low_half, zero, x_pair)], axis=0)
            m_pair = jnp.concatenate([ms[2 * pr], ms[2 * pr + 1]], axis=1)
            y_pairs.append(jnp.dot(m_pair, x_bd, preferred_element_type=F32))
        y_ref[0, :, gs] = (y_g + jnp.concatenate(y_pairs, axis=1)).astype(BF16)
        xw = (x_g.astype(F32) * w_rep).astype(BF16)
        s_g = jnp.dot(bt_g, xw, preferred_element_type=F32)
        dec_row = e_rep[q - 1:q, :] if d == 0 else e_rep[0:1, :]
        h_ref[0, d, :, gs] = ht_g * dec_row + s_g

    return group


def _ssd_scan_kernel(xf_ref, xb_ref, btf_ref, btb_ref, dtf_ref, dtb_ref, alog_ref, dskip_ref,
                     tab_ref, indew_ref, h0_ref, yf_ref, yb_ref, h_ref, tab_scr):
    @pl.when(pl.program_id(1) == 0)
    def _():
        h_ref[...] = h0_ref[...]
        tab_scr[...] = tab_ref[...]

    fwd_group = _scan_dir(0, xf_ref, btf_ref, dtf_ref, alog_ref, dskip_ref, tab_scr, indew_ref, yf_ref, h_ref)
    bwd_group = _scan_dir(1, xb_ref, btb_ref, dtb_ref, alog_ref, dskip_ref, tab_scr, indew_ref, yb_ref, h_ref)
    for g in range(SSD_GROUPS):
        fwd_group(g)
        bwd_group(g)


def _ssd_scan(xc, bt, dt, a_log, d_skip, h0):
    bsz, l, _ = xc.shape
    nc = l // SSD_CHUNK
    fwd = lambda b, c: (b, c, 0)
    bwd = lambda b, c: (b, nc - 1 - c, 0)
    fwd_t = lambda b, c: (b, 0, c)
    bwd_t = lambda b, c: (b, 0, nc - 1 - c)
    st_spec = pl.BlockSpec((1, 2, SSD_STATE, SSD_INNER), lambda b, c: (b, 0, 0, 0))
    seg_tab, ind_ew = _scan_tables()
    return pl.pallas_call(
        _ssd_scan_kernel,
        grid=(bsz, nc),
        in_specs=[
            pl.BlockSpec((1, SSD_CHUNK, SSD_INNER + SSD_GN), fwd),
            pl.BlockSpec((1, SSD_CHUNK, SSD_INNER + SSD_GN), bwd),
            pl.BlockSpec((1, SSD_GN, SSD_CHUNK), fwd_t),
            pl.BlockSpec((1, SSD_GN, SSD_CHUNK), bwd_t),
            pl.BlockSpec((1, SSD_CHUNK, 2 * SSD_HEADS), fwd),
            pl.BlockSpec((1, SSD_CHUNK, 2 * SSD_HEADS), bwd),
            _const_spec((2, SSD_HEADS)),
            _const_spec((1, SSD_INNER)),
            _const_spec(seg_tab.shape),
            _const_spec(ind_ew.shape),
            st_spec,
        ],
        out_specs=[
            pl.BlockSpec((1, SSD_CHUNK, SSD_INNER), fwd),
            pl.BlockSpec((1, SSD_CHUNK, SSD_INNER), bwd),
            st_spec,
        ],
        out_shape=[
            jax.ShapeDtypeStruct((bsz, l, SSD_INNER), BF16),
            jax.ShapeDtypeStruct((bsz, l, SSD_INNER), BF16),
            jax.ShapeDtypeStruct(h0.shape, F32),
        ],
        scratch_shapes=[pltpu.VMEM(seg_tab.shape, BF16)],
        compiler_params=_cparams("parallel", "arbitrary"),
        name="ssd_scan",
    )(xc, xc, bt, bt, dt, dt, a_log, d_skip, seg_tab, ind_ew, h0)


def _ssd_out_kernel(h_ref, yf_ref, yb_ref, z_ref, mod_ref, g_ref, ng_ref, wo_ref, o_ref, v_scr):
    gw = SSD_INNER // SSD_GROUPS
    for g in range(SSD_GROUPS):
        cols = slice(g * gw, (g + 1) * gw)
        y = (yf_ref[0, :, cols].astype(F32) + yb_ref[0, :, cols].astype(F32)) \
            * _silu(z_ref[0, :, cols].astype(F32))
        y = y * lax.rsqrt(jnp.mean(y * y, axis=-1, keepdims=True) + EPS)
        v_scr[:, cols] = (y * ng_ref[:, cols]).astype(BF16)
    out = jnp.dot(v_scr[...], wo_ref[...], preferred_element_type=F32)
    o_ref[0] = h_ref[0] + mod_ref[0, 5:6, :] * _rms(out, g_ref[3:4, :])


def _ssd_out(h, yf, yb, z, mod, g, norm_g, w_out, *, tm, mod_row):
    bsz, l, _ = h.shape
    tok = lambda w: pl.BlockSpec((1, tm, w), lambda b, t: (b, t, 0))
    return pl.pallas_call(
        _ssd_out_kernel,
        grid=(bsz, l // tm),
        in_specs=[
            tok(D_MODEL), tok(SSD_INNER), tok(SSD_INNER), tok(SSD_INNER),
            pl.BlockSpec((1, N_MOD, D_MODEL), lambda b, t: (mod_row(b), 0, 0)),
            _const_spec((6, D_MODEL)),
            _const_spec((1, SSD_INNER)),
            _const_spec((SSD_INNER, D_MODEL)),
        ],
        out_specs=tok(D_MODEL),
        out_shape=jax.ShapeDtypeStruct(h.shape, F32),
        scratch_shapes=[pltpu.VMEM((tm, SSD_INNER), BF16)],
        compiler_params=_cparams("parallel", "parallel"),
        name="ssd_out",
    )(h, yf, yb, z, mod, g, norm_g.reshape(1, -1), w_out)


def _gmlp_kernel(h_ref, mod_ref, g_ref, wu_ref, wv_ref, vg_ref, vb_ref, ws_ref, bs_ref, wo_ref, o_ref,
                 p_scr, *, tm):
    h = h_ref[0]
    u = (_rms(h, g_ref[2:3, :]) * (1.0 + mod_ref[0, 4:5, :]) + mod_ref[0, 3:4, :]).astype(BF16)
    gv = _gelu_tanh(jnp.dot(u, wv_ref[...], preferred_element_type=F32))
    mu = jnp.mean(gv, axis=-1, keepdims=True)
    var = jnp.mean(jnp.square(gv - mu), axis=-1, keepdims=True)
    gvn = ((gv - mu) * lax.rsqrt(var + EPS) * vg_ref[...] + vb_ref[...]).astype(BF16)
    for g in range(GM_GROUPS):
        cols = slice(g * GM_GROUP_DIM, (g + 1) * GM_GROUP_DIM)
        gu = _gelu_tanh(jnp.dot(u, wu_ref[:, cols], preferred_element_type=F32))
        for c in range(tm // GM_CHUNK):
            rows = slice(c * GM_CHUNK, (c + 1) * GM_CHUNK)
            s = jnp.dot(ws_ref[g], gvn[rows, cols], preferred_element_type=F32) + bs_ref[:, cols]
            p_scr[rows, cols] = (gu[rows, :] * s).astype(BF16)
    y = jnp.dot(p_scr[...], wo_ref[...], preferred_element_type=F32)
    o_ref[0] = h + mod_ref[0, 5:6, :] * _rms(y, g_ref[3:4, :])


def _gmlp(h, mod, g, w_in, v_g, v_b, w_s, b_s_full, w_out, *, tm, mod_row):
    bsz, l, _ = h.shape
    return pl.pallas_call(
        functools.partial(_gmlp_kernel, tm=tm),
        grid=(bsz, l // tm),
        in_specs=[
            pl.BlockSpec((1, tm, D_MODEL), lambda b, t: (b, t, 0)),
            pl.BlockSpec((1, N_MOD, D_MODEL), lambda b, t: (mod_row(b), 0, 0)),
            _const_spec((6, D_MODEL)),
            pl.BlockSpec((D_MODEL, GM_INNER), lambda b, t: (0, 0)),
            pl.BlockSpec((D_MODEL, GM_INNER), lambda b, t: (0, 1)),
            _const_spec((1, GM_INNER)),
            _const_spec((1, GM_INNER)),
            _const_spec((GM_GROUPS, GM_CHUNK, GM_CHUNK)),
            _const_spec((GM_CHUNK, GM_INNER)),
            _const_spec((GM_INNER, D_MODEL)),
        ],
        out_specs=pl.BlockSpec((1, tm, D_MODEL), lambda b, t: (b, t, 0)),
        out_shape=jax.ShapeDtypeStruct(h.shape, F32),
        scratch_shapes=[pltpu.VMEM((tm, GM_INNER), BF16)],
        compiler_params=_cparams("parallel", "parallel"),
        name="gmlp",
    )(h, mod, g, w_in, w_in, v_g.reshape(1, -1), v_b.reshape(1, -1), w_s, b_s_full, w_out)


def kernel(x, c, ctx, c_ctx, ada_w, ada_b, norm_g, ffn_w_in, ffn_w_out, ssd_w_in, ssd_conv_w, ssd_conv_b,
           ssd_dt_bias, ssd_A_log, ssd_D, ssd_norm_g, ssd_w_out, gm_w_in, gm_v_g, gm_v_b, gm_w_s, gm_b_s,
           gm_w_out):
    bsz, seq, _ = x.shape
    ctx_len = ctx.shape[1]
    cvec = jnp.zeros((MOD_ROWS, D_MODEL), F32).at[:bsz].set(c).at[bsz].set(c_ctx)
    mod_all = _ada_mod(cvec, ada_w, ada_b).reshape(DEPTH, MOD_ROWS, N_MOD, D_MODEL)

    x_row = lambda b: b
    ctx_row = lambda b: bsz
    tm_x = 512
    tm_c = ctx_len

    for i in range(DEPTH):
        use_ssd = (i % 2) == 0
        j = i // 2
        last = i == DEPTH - 1
        ctx_needed = (not last) or use_ssd
        ctx_full = not last
        mod = mod_all[i]
        g = norm_g[i]
        w1_in, w1_out = ffn_w_in[i, 0].astype(BF16), ffn_w_out[i, 0].astype(BF16)
        w2_in, w2_out = ffn_w_in[i, 1].astype(BF16), ffn_w_out[i, 1].astype(BF16)

        x = _ffn(x, mod, g, w1_in, w1_out, s=0, tm=tm_x, mod_row=x_row)
        if ctx_needed:
            ctx = _ffn(ctx, mod, g, w1_in, w1_out, s=0, tm=tm_c, mod_row=ctx_row)

        if use_ssd:
            w_in = ssd_w_in[j].astype(BF16)
            w_out = ssd_w_out[j].astype(BF16)
            d_skip = jnp.repeat(ssd_D[j], SSD_HEAD_DIM).reshape(1, SSD_INNER)
            dtb = ssd_dt_bias[j].reshape(-1)

            def branch(h, h0, tm, mod_row):
                z, xc, bt, dt = _ssd_in(h, mod, g, w_in, ssd_conv_w[j], ssd_conv_b[j], dtb,
                                        tm=tm, mod_row=mod_row)
                yf, yb, hT = _ssd_scan(xc, bt, dt, ssd_A_log[j], d_skip, h0)
                return yf, yb, z, hT

            h0 = jnp.zeros((bsz, 2, SSD_STATE, SSD_INNER), F32)
            cyf, cyb, cz, h_ctx = branch(ctx, h0, tm_c, ctx_row)
            xyf, xyb, xz, _ = branch(x, h_ctx, tm_x, x_row)
            x = _ssd_out(x, xyf, xyb, xz, mod, g, ssd_norm_g[j], w_out, tm=tm_x, mod_row=x_row)
            if ctx_full:
                ctx = _ssd_out(ctx, cyf, cyb, cz, mod, g, ssd_norm_g[j], w_out, tm=tm_c, mod_row=ctx_row)
        else:
            gw_in = gm_w_in[j].astype(BF16)
            gw_out = gm_w_out[j].astype(BF16)
            ws = gm_w_s[j].astype(BF16)
            bs_full = jnp.repeat(gm_b_s[j].T, GM_GROUP_DIM, axis=1)
            x = _gmlp(x, mod, g, gw_in, gm_v_g[j], gm_v_b[j], ws, bs_full, gw_out, tm=tm_x, mod_row=x_row)
            if ctx_full:
                ctx = _gmlp(ctx, mod, g, gw_in, gm_v_g[j], gm_v_b[j], ws, bs_full, gw_out,
                            tm=tm_c, mod_row=ctx_row)

        x = _ffn(x, mod, g, w2_in, w2_out, s=2, tm=tm_x, mod_row=x_row)
        if ctx_full:
            ctx = _ffn(ctx, mod, g, w2_in, w2_out, s=2, tm=tm_c, mod_row=ctx_row)
    return x
```

```python
import functools
import math

import jax
import jax.numpy as jnp
from jax import lax
from jax.experimental import pallas as pl
from jax.experimental.pallas import tpu as pltpu

F32 = jnp.float32
BF16 = jnp.bfloat16

D_MODEL = 1024
DEPTH = 2
N_MOD = 9
MACARON_W = 0.5
EPS = 1e-6
FFN_DIM = 2816

SSD_INNER = 2048
SSD_HEAD_DIM = 64
SSD_HEADS = 32
SSD_GROUPS = 8
SSD_HPG = 4
SSD_STATE = 128
SSD_CONV = 5
SSD_CHUNK = 128
SSD_GN = SSD_GROUPS * SSD_STATE
SSD_CONV_DIM = SSD_INNER + 2 * SSD_GN

GM_CHUNK = 128
GM_INNER = 2048
GM_GROUPS = 8
GM_GROUP_DIM = 256

MXU_TILE = 256
LANES = 128
SUBLANES = 8
BF16_ROWS = 16
ROW_BLOCK = 512
HALO = 16
VMEM_LIMIT = 56 * 1024 * 1024
MOD_ROWS = 16
LOG2E = math.log2(math.e)
NEG_BIG = -1e30


def _cparams(*sem):
    return pltpu.CompilerParams(dimension_semantics=sem, vmem_limit_bytes=VMEM_LIMIT)


def _rms(x, g):
    return x * lax.rsqrt(jnp.mean(x * x, axis=-1, keepdims=True) + EPS) * g


def _silu(x):
    return x * jax.nn.sigmoid(x)


def _gelu_tanh(x):
    k = -2.0 * LOG2E * math.sqrt(2.0 / math.pi)
    t = x * (k + (k * 0.044715) * (x * x))
    return x / (1.0 + jnp.exp2(t))


def _const_spec(shape):
    nd = len(shape)
    return pl.BlockSpec(shape, lambda *_: (0,) * nd)


def _ada_kernel(c_ref, w_ref, b_ref, o_ref):
    cs = _silu(c_ref[...]).astype(BF16)
    o_ref[...] = jnp.dot(cs, w_ref[...].astype(BF16), preferred_element_type=F32) + b_ref[...]


def _ada_mod(cvec, ada_w, ada_b):
    nd = N_MOD * D_MODEL
    bn = 1024
    return pl.pallas_call(
        _ada_kernel,
        grid=(DEPTH, nd // bn),
        in_specs=[
            pl.BlockSpec((MOD_ROWS, D_MODEL), lambda i, j: (0, 0)),
            pl.BlockSpec((None, D_MODEL, bn), lambda i, j: (i, 0, j)),
            pl.BlockSpec((None, 1, bn), lambda i, j: (i, 0, j)),
        ],
        out_specs=pl.BlockSpec((None, MOD_ROWS, bn), lambda i, j: (i, 0, j)),
        out_shape=jax.ShapeDtypeStruct((DEPTH, MOD_ROWS, nd), F32),
        compiler_params=_cparams("parallel", "parallel"),
        name="ada_mod",
    )(cvec, ada_w, ada_b.reshape(DEPTH, 1, nd))


def _ffn_kernel(h_ref, mod_ref, g_ref, wg_ref, wu_ref, wo_ref, o_ref, a_scr, *, s, tm):
    shift = mod_ref[0, 3 * s:3 * s + 1, :]
    scale = mod_ref[0, 3 * s + 1:3 * s + 2, :]
    gate = mod_ref[0, 3 * s + 2:3 * s + 3, :]
    rb = min(tm, ROW_BLOCK)
    for r in range(tm // rb):
        rows = slice(r * rb, (r + 1) * rb)
        h = h_ref[0, rows, :]
        u = (_rms(h, g_ref[2 * s:2 * s + 1, :]) * (1.0 + scale) + shift).astype(BF16)
        for c in range(FFN_DIM // MXU_TILE):
            cols = slice(c * MXU_TILE, (c + 1) * MXU_TILE)
            hg = jnp.dot(u, wg_ref[:, cols], preferred_element_type=F32)
            hu = jnp.dot(u, wu_ref[:, cols], preferred_element_type=F32)
            a_scr[rows, cols] = (_silu(hg) * hu).astype(BF16)
        y = jnp.dot(a_scr[rows, :], wo_ref[...], preferred_element_type=F32)
        o_ref[0, rows, :] = h + (MACARON_W * gate) * _rms(y, g_ref[2 * s + 1:2 * s + 2, :])


def _ffn(h, mod, g, w_in, w_out, *, layer, k, s, tm, mod_row):
    bsz, l, _ = h.shape
    resident = dict(pipeline_mode=pl.Buffered(1))
    return pl.pallas_call(
        functools.partial(_ffn_kernel, s=s, tm=tm),
        grid=(bsz, l // tm),
        in_specs=[
            pl.BlockSpec((1, tm, D_MODEL), lambda b, t: (b, t, 0)),
            pl.BlockSpec((1, N_MOD, D_MODEL), lambda b, t: (mod_row(b), 0, 0)),
            _const_spec((6, D_MODEL)),
            pl.BlockSpec((None, None, D_MODEL, FFN_DIM), lambda b, t: (layer, k, 0, 0), **resident),
            pl.BlockSpec((None, None, D_MODEL, FFN_DIM), lambda b, t: (layer, k, 0, 1), **resident),
            pl.BlockSpec((None, None, FFN_DIM, D_MODEL), lambda b, t: (layer, k, 0, 0), **resident),
        ],
        out_specs=pl.BlockSpec((1, tm, D_MODEL), lambda b, t: (b, t, 0)),
        out_shape=jax.ShapeDtypeStruct(h.shape, F32),
        scratch_shapes=[pltpu.VMEM((tm, FFN_DIM), BF16)],
        compiler_params=_cparams("parallel", "parallel"),
        name="ffn",
    )(h, mod, g, w_in, w_in, w_out)


def _ssd_in_kernel(xm_ref, xp_ref, xn_ref, mod_ref, g_ref, wz_ref, wxa_ref, wxb_ref, wdt_ref, cw_ref, cb_ref,
                   dtb_ref, z_ref, xc_ref, bt_ref, dt_ref, u_scr, p_scr, o_scr, *, tm):
    t = pl.program_id(1)
    nt = pl.num_programs(1)
    shift = mod_ref[0, 3:4, :]
    scale = mod_ref[0, 4:5, :]
    g_pre = g_ref[2:3, :]

    def prep(h):
        return _rms(h, g_pre) * (1.0 + scale) + shift

    u_scr[0:HALO, :] = jnp.where(t == 0, 0.0, prep(xp_ref[0])).astype(BF16)
    u_scr[HALO:HALO + tm, :] = prep(xm_ref[0]).astype(BF16)
    u_scr[HALO + tm:, :] = jnp.where(t == nt - 1, 0.0, prep(xn_ref[0])).astype(BF16)
    um = u_scr[HALO:HALO + tm, :]
    dt_raw = jnp.dot(um, wdt_ref[...], preferred_element_type=F32) + dtb_ref[...]
    dt_ref[0] = jnp.maximum(dt_raw, 0.0) + jnp.log1p(jnp.exp(-jnp.abs(dt_raw)))

    nblk = SSD_CONV_DIM // LANES
    npair = nblk // 2
    pairs_a = SSD_INNER // MXU_TILE
    groups8 = tm // (SUBLANES * SUBLANES)

    def project(pair):
        if pair < pairs_a:
            w = wxa_ref[:, pair * MXU_TILE:(pair + 1) * MXU_TILE]
        else:
            w = wxb_ref[:, (pair - pairs_a) * MXU_TILE:(pair - pairs_a + 1) * MXU_TILE]
        p = jnp.dot(u_scr[...], w, preferred_element_type=F32)
        p_scr[2 * pair] = p[:, :LANES]
        p_scr[2 * pair + 1] = p[:, LANES:]

    def conv_block(blk):
        cols = slice(blk * LANES, (blk + 1) * LANES)
        wk = [jnp.broadcast_to(cw_ref[k:k + 1, cols], (SUBLANES, LANES)) for k in range(SSD_CONV)]
        bias = jnp.broadcast_to(cb_ref[:, cols], (SUBLANES, LANES))
        slot = blk % 2
        for m in range(groups8):
            base = HALO - SSD_CONV // 2 + SUBLANES * SUBLANES * m
            slabs = [p_scr[blk, pl.ds(base + r, SUBLANES, stride=SUBLANES), :]
                     for r in range(SSD_CONV - 1)]
            for s in range(SUBLANES):
                slabs.append(p_scr[blk, pl.ds(base + s + SSD_CONV - 1, SUBLANES, stride=SUBLANES), :])
                acc = bias
                for k in range(SSD_CONV):
                    acc = acc + wk[k] * slabs[s + k]
                o_scr[slot, pl.ds(SUBLANES * SUBLANES * m + s, SUBLANES, stride=SUBLANES), :] = _silu(acc)
        val = o_scr[slot]
        nx = SSD_INNER // LANES
        nb = SSD_GN // LANES
        if blk < nx:
            xc_ref[0, :, cols] = val.astype(BF16)
        elif blk < nx + nb:
            rows = slice((blk - nx) * LANES, (blk - nx + 1) * LANES)
            bt_ref[0, rows, :] = val.T.astype(BF16)
        else:
            oc = slice((blk - nb) * LANES, (blk - nb + 1) * LANES)
            xc_ref[0, :, oc] = val.astype(BF16)

    project(0)
    z_ref[0] = jnp.dot(um, wz_ref[...], preferred_element_type=F32).astype(BF16)
    for pair in range(npair):
        conv_block(2 * pair)
        if pair + 1 < npair:
            project(pair + 1)
        conv_block(2 * pair + 1)


def _ssd_in(h, mod, g, w_in, conv_w, conv_b, dt_bias, *, tm, mod_row):
    bsz, l, _ = h.shape
    nh = tm // HALO
    last = l // HALO - 1
    wdt = w_in[:, SSD_INNER + SSD_CONV_DIM:]
    wblk = lambda j: pl.BlockSpec((D_MODEL, SSD_INNER), lambda b, t: (0, j))
    return pl.pallas_call(
        functools.partial(_ssd_in_kernel, tm=tm),
        grid=(bsz, l // tm),
        in_specs=[
            pl.BlockSpec((1, tm, D_MODEL), lambda b, t: (b, t, 0)),
            pl.BlockSpec((1, HALO, D_MODEL), lambda b, t: (b, jnp.maximum(t * nh - 1, 0), 0)),
            pl.BlockSpec((1, HALO, D_MODEL), lambda b, t: (b, jnp.minimum((t + 1) * nh, last), 0)),
            pl.BlockSpec((1, N_MOD, D_MODEL), lambda b, t: (mod_row(b), 0, 0)),
            _const_spec((6, D_MODEL)),
            wblk(0), wblk(1), wblk(2),
            _const_spec((D_MODEL, 2 * SSD_HEADS)),
            _const_spec((SSD_CONV, SSD_CONV_DIM)),
            _const_spec((1, SSD_CONV_DIM)),
            _const_spec((1, 2 * SSD_HEADS)),
        ],
        out_specs=[
            pl.BlockSpec((1, tm, SSD_INNER), lambda b, t: (b, t, 0)),
            pl.BlockSpec((1, tm, SSD_INNER + SSD_GN), lambda b, t: (b, t, 0)),
            pl.BlockSpec((1, SSD_GN, tm), lambda b, t: (b, 0, t)),
            pl.BlockSpec((1, tm, 2 * SSD_HEADS), lambda b, t: (b, t, 0)),
        ],
        out_shape=[
            jax.ShapeDtypeStruct((bsz, l, SSD_INNER), BF16),
            jax.ShapeDtypeStruct((bsz, l, SSD_INNER + SSD_GN), BF16),
            jax.ShapeDtypeStruct((bsz, SSD_GN, l), BF16),
            jax.ShapeDtypeStruct((bsz, l, 2 * SSD_HEADS), F32),
        ],
        scratch_shapes=[
            pltpu.VMEM((tm + 2 * HALO, D_MODEL), BF16),
            pltpu.VMEM((SSD_CONV_DIM // LANES, tm + 2 * HALO, LANES), F32),
            pltpu.VMEM((2, tm, LANES), F32),
        ],
        compiler_params=_cparams("parallel", "parallel"),
        name="ssd_in",
    )(h, h, h, mod, g, w_in, w_in, w_in, wdt, conv_w, conv_b.reshape(1, -1), dt_bias.reshape(1, -1))


def _split_bf16(v, n):
    parts = []
    for _ in range(n):
        p = v.astype(BF16).astype(F32)
        parts.append(p)
        v = v - p
    return parts


def _scan_tables():
    q = SSD_CHUNK
    r = jnp.arange(LANES)[:, None]
    ca = jnp.arange(SSD_HEADS * q)[None, :]
    ind_a = (((r % SSD_HEADS) == (ca // q)) & (r < 3 * SSD_HEADS)).astype(F32)
    j = ca % q
    mask_f = jnp.where(j > r, NEG_BIG, 0.0)
    mask_b = jnp.where(j < r, NEG_BIG, 0.0)
    seg_tab = jnp.stack([jnp.concatenate([ind_a, mask_f], axis=0),
                         jnp.concatenate([ind_a, mask_b], axis=0)]).astype(BF16)
    ce = jnp.arange(2 * SSD_INNER)[None, :]
    ind_ew = ((r % SSD_HEADS) == ((ce % SSD_INNER) // SSD_HEAD_DIM)) \
        & ((r // (2 * SSD_HEADS)) == (ce // SSD_INNER))
    return seg_tab, ind_ew.astype(BF16)


def _scan_dir(d, xc_ref, bt_ref, dt_ref, alog_ref, dskip_ref, tab_scr, indew_ref, y_ref, h_ref):
    q = SSD_CHUNK
    nh = SSD_HEADS
    dt = dt_ref[0][:, nh * d:nh * (d + 1)]
    a_neg = -jnp.exp(alog_ref[d:d + 1, :])
    row = lax.broadcasted_iota(jnp.int32, (q, q), 0)
    col = lax.broadcasted_iota(jnp.int32, (q, q), 1)
    tri = (row >= col) if d == 0 else (row <= col)
    pad_a = jnp.zeros((q, LANES - 3 * nh), F32)
    cs = jnp.dot(tri.astype(BF16),
                 jnp.concatenate(_split_bf16(dt * a_neg, 3) + [pad_a], axis=1).astype(BF16),
                 preferred_element_type=F32)
    a = cs[:, 0:nh] + cs[:, nh:2 * nh] + cs[:, 2 * nh:3 * nh]
    a_end = a[q - 1:q, :] if d == 0 else a[0:1, :]
    e_in = jnp.exp(a)
    w_end = dt * jnp.exp(a_end - a)
    r_nat = LOG2E * (jnp.maximum(jnp.log(dt), NEG_BIG) - a)
    r_t = jnp.concatenate([r_nat, jnp.zeros((q, LANES - nh), F32)], axis=1).T[0:nh]
    r_parts = _split_bf16(r_t, 3)
    prow = lax.broadcasted_iota(jnp.int32, (BF16_ROWS, q), 0)
    for hd in range(nh):
        tile = jnp.where(prow == 0, r_parts[0][hd:hd + 1, :],
                         jnp.where(prow == 1, r_parts[1][hd:hd + 1, :],
                                   jnp.where(prow == 2, r_parts[2][hd:hd + 1, :], 0.0)))
        tab_scr[d, 3 * nh:3 * nh + BF16_ROWS, q * hd:q * (hd + 1)] = tile.astype(BF16)
    ones3 = (lax.broadcasted_iota(jnp.int32, (q, LANES - 3 * nh), 1) < 3).astype(F32)
    eye = (row == col).astype(F32)
    seg_lhs = jnp.concatenate(_split_bf16(LOG2E * a, 3) + [ones3, eye], axis=1).astype(BF16)
    ew_parts = jnp.concatenate(_split_bf16(e_in, 2) + _split_bf16(w_end, 2), axis=1).astype(BF16)
    low_half = lax.broadcasted_iota(jnp.int32, (q, 2 * SSD_HEAD_DIM), 1) < SSD_HEAD_DIM

    gw = SSD_HPG * SSD_HEAD_DIM

    def group(g):
        gs = slice(gw * g, gw * (g + 1))
        c_g = xc_ref[0, :, SSD_INNER + SSD_STATE * g:SSD_INNER + SSD_STATE * (g + 1)]
        bt_g = bt_ref[0, SSD_STATE * g:SSD_STATE * (g + 1), :]
        x_g = xc_ref[0, :, gs]
        ht_g = h_ref[0, d, :, gs]
        cb = jnp.dot(c_g, bt_g, preferred_element_type=F32)
        seg = jnp.dot(seg_lhs, tab_scr[d, :, SSD_HPG * q * g:SSD_HPG * q * (g + 1)],
                      preferred_element_type=F32)
        e_rep = jnp.dot(ew_parts, indew_ref[:, gs], preferred_element_type=F32)
        w_rep = jnp.dot(ew_parts, indew_ref[:, SSD_INNER + gw * g:SSD_INNER + gw * (g + 1)],
                        preferred_element_type=F32)
        y_g = jnp.dot(c_g, ht_g.astype(BF16), preferred_element_type=F32) * e_rep
        if d == 0:
            y_g = y_g + dskip_ref[:, gs] * x_g.astype(F32)
        ms = [(cb * jnp.exp2(seg[:, q * k:q * (k + 1)])).astype(BF16) for k in range(SSD_HPG)]
        y_pairs = []
        for pr in range(SSD_HPG // 2):
            x_pair = x_g[:, 2 * SSD_HEAD_DIM * pr:2 * SSD_HEAD_DIM * (pr + 1)]
            zero = jnp.zeros_like(x_pair)
            x_bd = jnp.concatenate([jnp.where(low_half, x_pair, zero),
                                    jnp.where(low_half, zero, x_pair)], axis=0)
            m_pair = jnp.concatenate([ms[2 * pr], ms[2 * pr + 1]], axis=1)
            y_pairs.append(jnp.dot(m_pair, x_bd, preferred_element_type=F32))
        y_ref[0, :, gs] = (y_g + jnp.concatenate(y_pairs, axis=1)).astype(BF16)
        xw = (x_g.astype(F32) * w_rep).astype(BF16)
        s_g = jnp.dot(bt_g, xw, preferred_element_type=F32)
        dec_row = e_rep[q - 1:q, :] if d == 0 else e_rep[0:1, :]
        h_ref[0, d, :, gs] = ht_g * dec_row + s_g

    return group


def _ssd_scan_kernel(xf_ref, xb_ref, btf_ref, btb_ref, dtf_ref, dtb_ref, alog_ref, dskip_ref,
                     tab_ref, indew_ref, *rest, has_h0):
    if has_h0:
        h0_ref, yf_ref, yb_ref, h_ref, tab_scr = rest
    else:
        yf_ref, yb_ref, h_ref, tab_scr = rest

    @pl.when(pl.program_id(1) == 0)
    def _():
        h_ref[...] = h0_ref[...] if has_h0 else jnp.zeros(h_ref.shape, F32)
        tab_scr[...] = tab_ref[...]

    fwd_group = _scan_dir(0, xf_ref, btf_ref, dtf_ref, alog_ref, dskip_ref, tab_scr, indew_ref, yf_ref, h_ref)
    bwd_group = _scan_dir(1, xb_ref, btb_ref, dtb_ref, alog_ref, dskip_ref, tab_scr, indew_ref, yb_ref, h_ref)
    for g in range(SSD_GROUPS):
        fwd_group(g)
        bwd_group(g)


def _ssd_scan(xc, bt, dt, a_log, d_skip, h0=None):
    bsz, l, _ = xc.shape
    has_h0 = h0 is not None
    nc = l // SSD_CHUNK
    fwd = lambda b, c: (b, c, 0)
    bwd = lambda b, c: (b, nc - 1 - c, 0)
    fwd_t = lambda b, c: (b, 0, c)
    bwd_t = lambda b, c: (b, 0, nc - 1 - c)
    st_spec = pl.BlockSpec((1, 2, SSD_STATE, SSD_INNER), lambda b, c: (b, 0, 0, 0))
    seg_tab, ind_ew = _scan_tables()
    return pl.pallas_call(
        functools.partial(_ssd_scan_kernel, has_h0=has_h0),
        grid=(bsz, nc),
        in_specs=[
            pl.BlockSpec((1, SSD_CHUNK, SSD_INNER + SSD_GN), fwd),
            pl.BlockSpec((1, SSD_CHUNK, SSD_INNER + SSD_GN), bwd),
            pl.BlockSpec((1, SSD_GN, SSD_CHUNK), fwd_t),
            pl.BlockSpec((1, SSD_GN, SSD_CHUNK), bwd_t),
            pl.BlockSpec((1, SSD_CHUNK, 2 * SSD_HEADS), fwd),
            pl.BlockSpec((1, SSD_CHUNK, 2 * SSD_HEADS), bwd),
            _const_spec((2, SSD_HEADS)),
            _const_spec((1, SSD_INNER)),
            _const_spec(seg_tab.shape),
            _const_spec(ind_ew.shape),
        ] + ([st_spec] if has_h0 else []),
        out_specs=[
            pl.BlockSpec((1, SSD_CHUNK, SSD_INNER), fwd),
            pl.BlockSpec((1, SSD_CHUNK, SSD_INNER), bwd),
            st_spec,
        ],
        out_shape=[
            jax.ShapeDtypeStruct((bsz, l, SSD_INNER), BF16),
            jax.ShapeDtypeStruct((bsz, l, SSD_INNER), BF16),
            jax.ShapeDtypeStruct((bsz, 2, SSD_STATE, SSD_INNER), F32),
        ],
        scratch_shapes=[pltpu.VMEM(seg_tab.shape, BF16)],
        compiler_params=_cparams("parallel", "arbitrary"),
        name="ssd_scan",
    )(xc, xc, bt, bt, dt, dt, a_log, d_skip, seg_tab, ind_ew, *([h0] if has_h0 else []))


def _ssd_out_kernel(h_ref, yf_ref, yb_ref, z_ref, mod_ref, g_ref, ng_ref, wo_ref, o_ref, v_scr):
    gw = SSD_INNER // SSD_GROUPS
    for g in range(SSD_GROUPS):
        cols = slice(g * gw, (g + 1) * gw)
        y = (yf_ref[0, :, cols].astype(F32) + yb_ref[0, :, cols].astype(F32)) \
            * _silu(z_ref[0, :, cols].astype(F32))
        y = y * lax.rsqrt(jnp.mean(y * y, axis=-1, keepdims=True) + EPS)
        v_scr[:, cols] = (y * ng_ref[:, cols]).astype(BF16)
    out = jnp.dot(v_scr[...], wo_ref[...], preferred_element_type=F32)
    o_ref[0] = h_ref[0] + mod_ref[0, 5:6, :] * _rms(out, g_ref[3:4, :])


def _ssd_out(h, yf, yb, z, mod, g, norm_g, w_out, *, tm, mod_row):
    bsz, l, _ = h.shape
    tok = lambda w: pl.BlockSpec((1, tm, w), lambda b, t: (b, t, 0))
    return pl.pallas_call(
        _ssd_out_kernel,
        grid=(bsz, l // tm),
        in_specs=[
            tok(D_MODEL), tok(SSD_INNER), tok(SSD_INNER), tok(SSD_INNER),
            pl.BlockSpec((1, N_MOD, D_MODEL), lambda b, t: (mod_row(b), 0, 0)),
            _const_spec((6, D_MODEL)),
            _const_spec((1, SSD_INNER)),
            _const_spec((SSD_INNER, D_MODEL)),
        ],
        out_specs=tok(D_MODEL),
        out_shape=jax.ShapeDtypeStruct(h.shape, F32),
        scratch_shapes=[pltpu.VMEM((tm, SSD_INNER), BF16)],
        compiler_params=_cparams("parallel", "parallel"),
        name="ssd_out",
    )(h, yf, yb, z, mod, g, norm_g.reshape(1, -1), w_out)


def _gmlp_kernel(h_ref, mod_ref, g_ref, wu_ref, wv_ref, vg_ref, vb_ref, ws_ref, bs_ref, wo_ref, o_ref,
                 p_scr, *, tm):
    h = h_ref[0]
    u = (_rms(h, g_ref[2:3, :]) * (1.0 + mod_ref[0, 4:5, :]) + mod_ref[0, 3:4, :]).astype(BF16)
    gv = _gelu_tanh(jnp.dot(u, wv_ref[...], preferred_element_type=F32))
    mu = jnp.mean(gv, axis=-1, keepdims=True)
    var = jnp.mean(jnp.square(gv - mu), axis=-1, keepdims=True)
    gvn = ((gv - mu) * lax.rsqrt(var + EPS) * vg_ref[...] + vb_ref[...]).astype(BF16)
    for g in range(GM_GROUPS):
        cols = slice(g * GM_GROUP_DIM, (g + 1) * GM_GROUP_DIM)
        gu = _gelu_tanh(jnp.dot(u, wu_ref[:, cols], preferred_element_type=F32))
        for c in range(tm // GM_CHUNK):
            rows = slice(c * GM_CHUNK, (c + 1) * GM_CHUNK)
            s = jnp.dot(ws_ref[g], gvn[rows, cols], preferred_element_type=F32) + bs_ref[:, cols]
            p_scr[rows, cols] = (gu[rows, :] * s).astype(BF16)
    y = jnp.dot(p_scr[...], wo_ref[...], preferred_element_type=F32)
    o_ref[0] = h + mod_ref[0, 5:6, :] * _rms(y, g_ref[3:4, :])


def _gmlp(h, mod, g, w_in, v_g, v_b, w_s, b_s_full, w_out, *, tm, mod_row):
    bsz, l, _ = h.shape
    return pl.pallas_call(
        functools.partial(_gmlp_kernel, tm=tm),
        grid=(bsz, l // tm),
        in_specs=[
            pl.BlockSpec((1, tm, D_MODEL), lambda b, t: (b, t, 0)),
            pl.BlockSpec((1, N_MOD, D_MODEL), lambda b, t: (mod_row(b), 0, 0)),
            _const_spec((6, D_MODEL)),
            pl.BlockSpec((D_MODEL, GM_INNER), lambda b, t: (0, 0)),
            pl.BlockSpec((D_MODEL, GM_INNER), lambda b, t: (0, 1)),
            _const_spec((1, GM_INNER)),
            _const_spec((1, GM_INNER)),
            _const_spec((GM_GROUPS, GM_CHUNK, GM_CHUNK)),
            _const_spec((GM_CHUNK, GM_INNER)),
            _const_spec((GM_INNER, D_MODEL)),
        ],
        out_specs=pl.BlockSpec((1, tm, D_MODEL), lambda b, t: (b, t, 0)),
        out_shape=jax.ShapeDtypeStruct(h.shape, F32),
        scratch_shapes=[pltpu.VMEM((tm, GM_INNER), BF16)],
        compiler_params=_cparams("parallel", "parallel"),
        name="gmlp",
    )(h, mod, g, w_in, w_in, v_g.reshape(1, -1), v_b.reshape(1, -1), w_s, b_s_full, w_out)


def kernel(x, c, ctx, c_ctx, ada_w, ada_b, norm_g, ffn_w_in, ffn_w_out, ssd_w_in, ssd_conv_w, ssd_conv_b,
           ssd_dt_bias, ssd_A_log, ssd_D, ssd_norm_g, ssd_w_out, gm_w_in, gm_v_g, gm_v_b, gm_w_s, gm_b_s,
           gm_w_out):
    bsz, seq, _ = x.shape
    ctx_len = ctx.shape[1]
    cvec = jnp.zeros((MOD_ROWS, D_MODEL), F32).at[:bsz].set(c).at[bsz].set(c_ctx)
    mod_all = _ada_mod(cvec, ada_w, ada_b).reshape(DEPTH, MOD_ROWS, N_MOD, D_MODEL)

    x_row = lambda b: b
    ctx_row = lambda b: bsz
    tm_x = 512
    tm_ffn = 1024
    tm_c = ctx_len
    ffn_in = ffn_w_in.astype(BF16)
    ffn_out = ffn_w_out.astype(BF16)

    for i in range(DEPTH):
        use_ssd = (i % 2) == 0
        j = i // 2
        last = i == DEPTH - 1
        ctx_needed = (not last) or use_ssd
        ctx_full = not last
        mod = mod_all[i]
        g = norm_g[i]

        x = _ffn(x, mod, g, ffn_in, ffn_out, layer=i, k=0, s=0, tm=tm_ffn, mod_row=x_row)
        if ctx_needed:
            ctx = _ffn(ctx, mod, g, ffn_in, ffn_out, layer=i, k=0, s=0, tm=tm_c, mod_row=ctx_row)

        if use_ssd:
            w_in = ssd_w_in[j].astype(BF16)
            w_out = ssd_w_out[j].astype(BF16)
            d_skip = jnp.repeat(ssd_D[j], SSD_HEAD_DIM).reshape(1, SSD_INNER)
            dtb = ssd_dt_bias[j].reshape(-1)

            def branch(h, h0, tm, mod_row):
                z, xc, bt, dt = _ssd_in(h, mod, g, w_in, ssd_conv_w[j], ssd_conv_b[j], dtb,
                                        tm=tm, mod_row=mod_row)
                yf, yb, hT = _ssd_scan(xc, bt, dt, ssd_A_log[j], d_skip, h0)
                return yf, yb, z, hT

            cyf, cyb, cz, h_ctx = branch(ctx, None, tm_c, ctx_row)
            xyf, xyb, xz, _ = branch(x, h_ctx, tm_x, x_row)
            x = _ssd_out(x, xyf, xyb, xz, mod, g, ssd_norm_g[j], w_out, tm=tm_x, mod_row=x_row)
            if ctx_full:
                ctx = _ssd_out(ctx, cyf, cyb, cz, mod, g, ssd_norm_g[j], w_out, tm=tm_c, mod_row=ctx_row)
        else:
            gw_in = gm_w_in[j].astype(BF16)
            gw_out = gm_w_out[j].astype(BF16)
            ws = gm_w_s[j].astype(BF16)
            bs_full = jnp.repeat(gm_b_s[j].T, GM_GROUP_DIM, axis=1)
            x = _gmlp(x, mod, g, gw_in, gm_v_g[j], gm_v_b[j], ws, bs_full, gw_out, tm=tm_x, mod_row=x_row)
            if ctx_full:
                ctx = _gmlp(ctx, mod, g, gw_in, gm_v_g[j], gm_v_b[j], ws, bs_full, gw_out,
                            tm=tm_c, mod_row=ctx_row)

        x = _ffn(x, mod, g, ffn_in, ffn_out, layer=i, k=1, s=2, tm=tm_ffn, mod_row=x_row)
        if ctx_full:
            ctx = _ffn(ctx, mod, g, ffn_in, ffn_out, layer=i, k=1, s=2, tm=tm_c, mod_row=ctx_row)
    return x
```

```python
import functools
import math

import jax
import jax.numpy as jnp
from jax import lax
from jax.experimental import pallas as pl
from jax.experimental.pallas import tpu as pltpu

F32 = jnp.float32
BF16 = jnp.bfloat16

D_MODEL = 1024
DEPTH = 2
N_MOD = 9
MACARON_W = 0.5
EPS = 1e-6
FFN_DIM = 2816

SSD_INNER = 2048
SSD_HEAD_DIM = 64
SSD_HEADS = 32
SSD_GROUPS = 8
SSD_HPG = 4
SSD_STATE = 128
SSD_CONV = 5
SSD_CHUNK = 128
SSD_GN = SSD_GROUPS * SSD_STATE
SSD_CONV_DIM = SSD_INNER + 2 * SSD_GN

GM_CHUNK = 128
GM_INNER = 2048
GM_GROUPS = 8
GM_GROUP_DIM = 256

MXU_TILE = 256
LANES = 128
SUBLANES = 8
BF16_ROWS = 16
SCAN_CHUNKS = 2
ROW_BLOCK = 512
HALO = 16
VMEM_LIMIT = 56 * 1024 * 1024
MOD_ROWS = 16
LOG2E = math.log2(math.e)
NEG_BIG = -1e30


def _cparams(*sem):
    return pltpu.CompilerParams(dimension_semantics=sem, vmem_limit_bytes=VMEM_LIMIT)


def _rms(x, g):
    return x * lax.rsqrt(jnp.mean(x * x, axis=-1, keepdims=True) + EPS) * g


def _silu(x):
    return x * jax.nn.sigmoid(x)


def _gelu_tanh(x):
    k = -2.0 * LOG2E * math.sqrt(2.0 / math.pi)
    t = x * (k + (k * 0.044715) * (x * x))
    return x / (1.0 + jnp.exp2(t))


def _const_spec(shape):
    nd = len(shape)
    return pl.BlockSpec(shape, lambda *_: (0,) * nd)


def _ada_kernel(c_ref, w_ref, b_ref, o_ref):
    cs = _silu(c_ref[...]).astype(BF16)
    o_ref[...] = jnp.dot(cs, w_ref[...].astype(BF16), preferred_element_type=F32) + b_ref[...]


def _ada_mod(cvec, ada_w, ada_b):
    nd = N_MOD * D_MODEL
    bn = 1024
    return pl.pallas_call(
        _ada_kernel,
        grid=(DEPTH, nd // bn),
        in_specs=[
            pl.BlockSpec((MOD_ROWS, D_MODEL), lambda i, j: (0, 0)),
            pl.BlockSpec((None, D_MODEL, bn), lambda i, j: (i, 0, j)),
            pl.BlockSpec((None, 1, bn), lambda i, j: (i, 0, j)),
        ],
        out_specs=pl.BlockSpec((None, MOD_ROWS, bn), lambda i, j: (i, 0, j)),
        out_shape=jax.ShapeDtypeStruct((DEPTH, MOD_ROWS, nd), F32),
        compiler_params=_cparams("parallel", "parallel"),
        name="ada_mod",
    )(cvec, ada_w, ada_b.reshape(DEPTH, 1, nd))


def _ffn_kernel(h_ref, mod_ref, g_ref, wg_ref, wu_ref, wo_ref, o_ref, a_scr, *, s, tm):
    shift = mod_ref[0, 3 * s:3 * s + 1, :]
    scale = mod_ref[0, 3 * s + 1:3 * s + 2, :]
    gate = mod_ref[0, 3 * s + 2:3 * s + 3, :]
    rb = min(tm, ROW_BLOCK)
    for r in range(tm // rb):
        rows = slice(r * rb, (r + 1) * rb)
        h = h_ref[0, rows, :]
        u = (_rms(h, g_ref[2 * s:2 * s + 1, :]) * (1.0 + scale) + shift).astype(BF16)
        for c in range(FFN_DIM // MXU_TILE):
            cols = slice(c * MXU_TILE, (c + 1) * MXU_TILE)
            hg = jnp.dot(u, wg_ref[:, cols], preferred_element_type=F32)
            hu = jnp.dot(u, wu_ref[:, cols], preferred_element_type=F32)
            a_scr[rows, cols] = (_silu(hg) * hu).astype(BF16)
        y = jnp.dot(a_scr[rows, :], wo_ref[...], preferred_element_type=F32)
        o_ref[0, rows, :] = h + (MACARON_W * gate) * _rms(y, g_ref[2 * s + 1:2 * s + 2, :])


def _ffn(h, mod, g, w_in, w_out, *, layer, k, s, tm, mod_row):
    bsz, l, _ = h.shape
    resident = dict(pipeline_mode=pl.Buffered(1))
    return pl.pallas_call(
        functools.partial(_ffn_kernel, s=s, tm=tm),
        grid=(bsz, l // tm),
        in_specs=[
            pl.BlockSpec((1, tm, D_MODEL), lambda b, t: (b, t, 0)),
            pl.BlockSpec((1, N_MOD, D_MODEL), lambda b, t: (mod_row(b), 0, 0)),
            _const_spec((6, D_MODEL)),
            pl.BlockSpec((None, None, D_MODEL, FFN_DIM), lambda b, t: (layer, k, 0, 0), **resident),
            pl.BlockSpec((None, None, D_MODEL, FFN_DIM), lambda b, t: (layer, k, 0, 1), **resident),
            pl.BlockSpec((None, None, FFN_DIM, D_MODEL), lambda b, t: (layer, k, 0, 0), **resident),
        ],
        out_specs=pl.BlockSpec((1, tm, D_MODEL), lambda b, t: (b, t, 0)),
        out_shape=jax.ShapeDtypeStruct(h.shape, F32),
        scratch_shapes=[pltpu.VMEM((tm, FFN_DIM), BF16)],
        compiler_params=_cparams("parallel", "parallel"),
        name="ffn",
    )(h, mod, g, w_in, w_in, w_out)


def _ssd_in_kernel(xm_ref, xp_ref, xn_ref, mod_ref, g_ref, wz_ref, wxa_ref, wxb_ref, wdt_ref, cw_ref, cb_ref,
                   dtb_ref, z_ref, xc_ref, bt_ref, dt_ref, u_scr, p_scr, o_scr, *, tm):
    t = pl.program_id(1)
    nt = pl.num_programs(1)
    shift = mod_ref[0, 3:4, :]
    scale = mod_ref[0, 4:5, :]
    g_pre = g_ref[2:3, :]

    def prep(h):
        return _rms(h, g_pre) * (1.0 + scale) + shift

    u_scr[0:HALO, :] = jnp.where(t == 0, 0.0, prep(xp_ref[0])).astype(BF16)
    u_scr[HALO:HALO + tm, :] = prep(xm_ref[0]).astype(BF16)
    u_scr[HALO + tm:, :] = jnp.where(t == nt - 1, 0.0, prep(xn_ref[0])).astype(BF16)
    um = u_scr[HALO:HALO + tm, :]
    dt_raw = jnp.dot(um, wdt_ref[...], preferred_element_type=F32) + dtb_ref[...]
    dt_ref[0] = jnp.maximum(dt_raw, 0.0) + jnp.log1p(jnp.exp(-jnp.abs(dt_raw)))

    nblk = SSD_CONV_DIM // LANES
    npair = nblk // 2
    pairs_a = SSD_INNER // MXU_TILE
    groups8 = tm // (SUBLANES * SUBLANES)

    def project(pair):
        if pair < pairs_a:
            w = wxa_ref[:, pair * MXU_TILE:(pair + 1) * MXU_TILE]
        else:
            w = wxb_ref[:, (pair - pairs_a) * MXU_TILE:(pair - pairs_a + 1) * MXU_TILE]
        p = jnp.dot(u_scr[...], w, preferred_element_type=F32)
        p_scr[2 * pair] = p[:, :LANES]
        p_scr[2 * pair + 1] = p[:, LANES:]

    def conv_block(blk):
        cols = slice(blk * LANES, (blk + 1) * LANES)
        wk = [jnp.broadcast_to(cw_ref[k:k + 1, cols], (SUBLANES, LANES)) for k in range(SSD_CONV)]
        bias = jnp.broadcast_to(cb_ref[:, cols], (SUBLANES, LANES))
        slot = blk % 2
        for m in range(groups8):
            base = HALO - SSD_CONV // 2 + SUBLANES * SUBLANES * m
            slabs = [p_scr[blk, pl.ds(base + r, SUBLANES, stride=SUBLANES), :]
                     for r in range(SSD_CONV - 1)]
            for s in range(SUBLANES):
                slabs.append(p_scr[blk, pl.ds(base + s + SSD_CONV - 1, SUBLANES, stride=SUBLANES), :])
                acc = bias
                for k in range(SSD_CONV):
                    acc = acc + wk[k] * slabs[s + k]
                o_scr[slot, pl.ds(SUBLANES * SUBLANES * m + s, SUBLANES, stride=SUBLANES), :] = _silu(acc)
        val = o_scr[slot]
        nx = SSD_INNER // LANES
        nb = SSD_GN // LANES
        if blk < nx:
            xc_ref[0, :, cols] = val.astype(BF16)
        elif blk < nx + nb:
            rows = slice((blk - nx) * LANES, (blk - nx + 1) * LANES)
            bt_ref[0, rows, :] = val.T.astype(BF16)
        else:
            oc = slice((blk - nb) * LANES, (blk - nb + 1) * LANES)
            xc_ref[0, :, oc] = val.astype(BF16)

    project(0)
    z_ref[0] = jnp.dot(um, wz_ref[...], preferred_element_type=F32).astype(BF16)
    for pair in range(npair):
        conv_block(2 * pair)
        if pair + 1 < npair:
            project(pair + 1)
        conv_block(2 * pair + 1)


def _ssd_in(h, mod, g, w_in, conv_w, conv_b, dt_bias, *, tm, mod_row):
    bsz, l, _ = h.shape
    nh = tm // HALO
    last = l // HALO - 1
    dt_pad = LANES - 2 * SSD_HEADS
    wdt = jnp.pad(w_in[:, SSD_INNER + SSD_CONV_DIM:], ((0, 0), (0, dt_pad)))
    dt_bias = jnp.pad(dt_bias, (0, dt_pad))
    wblk = lambda j: pl.BlockSpec((D_MODEL, SSD_INNER), lambda b, t: (0, j))
    return pl.pallas_call(
        functools.partial(_ssd_in_kernel, tm=tm),
        grid=(bsz, l // tm),
        in_specs=[
            pl.BlockSpec((1, tm, D_MODEL), lambda b, t: (b, t, 0)),
            pl.BlockSpec((1, HALO, D_MODEL), lambda b, t: (b, jnp.maximum(t * nh - 1, 0), 0)),
            pl.BlockSpec((1, HALO, D_MODEL), lambda b, t: (b, jnp.minimum((t + 1) * nh, last), 0)),
            pl.BlockSpec((1, N_MOD, D_MODEL), lambda b, t: (mod_row(b), 0, 0)),
            _const_spec((6, D_MODEL)),
            wblk(0), wblk(1), wblk(2),
            _const_spec((D_MODEL, LANES)),
            _const_spec((SSD_CONV, SSD_CONV_DIM)),
            _const_spec((1, SSD_CONV_DIM)),
            _const_spec((1, LANES)),
        ],
        out_specs=[
            pl.BlockSpec((1, tm, SSD_INNER), lambda b, t: (b, t, 0)),
            pl.BlockSpec((1, tm, SSD_INNER + SSD_GN), lambda b, t: (b, t, 0)),
            pl.BlockSpec((1, SSD_GN, tm), lambda b, t: (b, 0, t)),
            pl.BlockSpec((1, tm, LANES), lambda b, t: (b, t, 0)),
        ],
        out_shape=[
            jax.ShapeDtypeStruct((bsz, l, SSD_INNER), BF16),
            jax.ShapeDtypeStruct((bsz, l, SSD_INNER + SSD_GN), BF16),
            jax.ShapeDtypeStruct((bsz, SSD_GN, l), BF16),
            jax.ShapeDtypeStruct((bsz, l, LANES), F32),
        ],
        scratch_shapes=[
            pltpu.VMEM((tm + 2 * HALO, D_MODEL), BF16),
            pltpu.VMEM((SSD_CONV_DIM // LANES, tm + 2 * HALO, LANES), F32),
            pltpu.VMEM((2, tm, LANES), F32),
        ],
        compiler_params=_cparams("parallel", "parallel"),
        name="ssd_in",
    )(h, h, h, mod, g, w_in, w_in, w_in, wdt, conv_w, conv_b.reshape(1, -1), dt_bias.reshape(1, -1))


def _split_bf16(v, n):
    parts = []
    for _ in range(n):
        p = v.astype(BF16).astype(F32)
        parts.append(p)
        v = v - p
    return parts


def _scan_tables():
    q = SSD_CHUNK
    r = jnp.arange(LANES)[:, None]
    ca = jnp.arange(SSD_HEADS * q)[None, :]
    ind_a = (((r % SSD_HEADS) == (ca // q)) & (r < 3 * SSD_HEADS)).astype(F32)
    j = ca % q
    mask_f = jnp.where(j > r, NEG_BIG, 0.0)
    mask_b = jnp.where(j < r, NEG_BIG, 0.0)
    seg_tab = jnp.stack([jnp.concatenate([ind_a, mask_f], axis=0),
                         jnp.concatenate([ind_a, mask_b], axis=0)]).astype(BF16)
    ce = jnp.arange(2 * SSD_INNER)[None, :]
    ind_ew = ((r % SSD_HEADS) == ((ce % SSD_INNER) // SSD_HEAD_DIM)) \
        & ((r // (2 * SSD_HEADS)) == (ce // SSD_INNER))
    return seg_tab, ind_ew.astype(BF16)


def _scan_dir(d, ro, xc_ref, bt_ref, dt_ref, alog_ref, dskip_ref, tab_scr, indew_ref, y_ref, h_ref):
    q = SSD_CHUNK
    nh = SSD_HEADS
    rws = slice(ro, ro + q)
    dt = dt_ref[0, rws, nh * d:nh * (d + 1)]
    a_neg = -jnp.exp(alog_ref[d:d + 1, :])
    row = lax.broadcasted_iota(jnp.int32, (q, q), 0)
    col = lax.broadcasted_iota(jnp.int32, (q, q), 1)
    tri = (row >= col) if d == 0 else (row <= col)
    pad_a = jnp.zeros((q, LANES - 3 * nh), F32)
    cs = jnp.dot(tri.astype(BF16),
                 jnp.concatenate(_split_bf16(dt * a_neg, 3) + [pad_a], axis=1).astype(BF16),
                 preferred_element_type=F32)
    a = cs[:, 0:nh] + cs[:, nh:2 * nh] + cs[:, 2 * nh:3 * nh]
    a_end = a[q - 1:q, :] if d == 0 else a[0:1, :]
    e_in = jnp.exp(a)
    w_end = dt * jnp.exp(a_end - a)
    r_nat = LOG2E * (jnp.maximum(jnp.log(dt), NEG_BIG) - a)
    r_t = jnp.concatenate([r_nat, jnp.zeros((q, LANES - nh), F32)], axis=1).T[0:nh]
    r_parts = _split_bf16(r_t, 3)
    prow = lax.broadcasted_iota(jnp.int32, (BF16_ROWS, q), 0)
    for hd in range(nh):
        tile = jnp.where(prow == 0, r_parts[0][hd:hd + 1, :],
                         jnp.where(prow == 1, r_parts[1][hd:hd + 1, :],
                                   jnp.where(prow == 2, r_parts[2][hd:hd + 1, :], 0.0)))
        tab_scr[d, 3 * nh:3 * nh + BF16_ROWS, q * hd:q * (hd + 1)] = tile.astype(BF16)
    ones3 = (lax.broadcasted_iota(jnp.int32, (q, LANES - 3 * nh), 1) < 3).astype(F32)
    eye = (row == col).astype(F32)
    seg_lhs = jnp.concatenate(_split_bf16(LOG2E * a, 3) + [ones3, eye], axis=1).astype(BF16)
    ew_parts = jnp.concatenate(_split_bf16(e_in, 2) + _split_bf16(w_end, 2), axis=1).astype(BF16)
    low_half = lax.broadcasted_iota(jnp.int32, (q, 2 * SSD_HEAD_DIM), 1) < SSD_HEAD_DIM

    gw = SSD_HPG * SSD_HEAD_DIM

    def group(g):
        gs = slice(gw * g, gw * (g + 1))
        c_g = xc_ref[0, rws, SSD_INNER + SSD_STATE * g:SSD_INNER + SSD_STATE * (g + 1)]
        bt_g = bt_ref[0, SSD_STATE * g:SSD_STATE * (g + 1), rws]
        x_g = xc_ref[0, rws, gs]
        ht_g = h_ref[0, d, :, gs]
        cb = jnp.dot(c_g, bt_g, preferred_element_type=F32)
        seg = jnp.dot(seg_lhs, tab_scr[d, :, SSD_HPG * q * g:SSD_HPG * q * (g + 1)],
                      preferred_element_type=F32)
        e_rep = jnp.dot(ew_parts, indew_ref[:, gs], preferred_element_type=F32)
        w_rep = jnp.dot(ew_parts, indew_ref[:, SSD_INNER + gw * g:SSD_INNER + gw * (g + 1)],
                        preferred_element_type=F32)
        y_g = jnp.dot(c_g, ht_g.astype(BF16), preferred_element_type=F32) * e_rep
        if d == 0:
            y_g = y_g + dskip_ref[:, gs] * x_g.astype(F32)
        ms = [(cb * jnp.exp2(seg[:, q * k:q * (k + 1)])).astype(BF16) for k in range(SSD_HPG)]
        y_pairs = []
        for pr in range(SSD_HPG // 2):
            x_pair = x_g[:, 2 * SSD_HEAD_DIM * pr:2 * SSD_HEAD_DIM * (pr + 1)]
            zero = jnp.zeros_like(x_pair)
            x_bd = jnp.concatenate([jnp.where(low_half, x_pair, zero),
                                    jnp.where(low_half, zero, x_pair)], axis=0)
            m_pair = jnp.concatenate([ms[2 * pr], ms[2 * pr + 1]], axis=1)
            y_pairs.append(jnp.dot(m_pair, x_bd, preferred_element_type=F32))
        y_ref[0, rws, gs] = (y_g + jnp.concatenate(y_pairs, axis=1)).astype(BF16)
        xw = (x_g.astype(F32) * w_rep).astype(BF16)
        s_g = jnp.dot(bt_g, xw, preferred_element_type=F32)
        dec_row = e_rep[q - 1:q, :] if d == 0 else e_rep[0:1, :]
        h_ref[0, d, :, gs] = ht_g * dec_row + s_g

    return group


def _ssd_scan_kernel(xf_ref, xb_ref, btf_ref, btb_ref, dtf_ref, dtb_ref, alog_ref, dskip_ref,
                     tab_ref, indew_ref, *rest, has_h0):
    if has_h0:
        h0_ref, yf_ref, yb_ref, h_ref, tab_scr = rest
    else:
        yf_ref, yb_ref, h_ref, tab_scr = rest

    @pl.when(pl.program_id(1) == 0)
    def _():
        h_ref[...] = h0_ref[...] if has_h0 else jnp.zeros(h_ref.shape, F32)
        tab_scr[...] = tab_ref[...]

    for sub in range(SCAN_CHUNKS):
        ro_f = sub * SSD_CHUNK
        ro_b = (SCAN_CHUNKS - 1 - sub) * SSD_CHUNK
        fwd_group = _scan_dir(0, ro_f, xf_ref, btf_ref, dtf_ref, alog_ref, dskip_ref, tab_scr, indew_ref,
                              yf_ref, h_ref)
        bwd_group = _scan_dir(1, ro_b, xb_ref, btb_ref, dtb_ref, alog_ref, dskip_ref, tab_scr, indew_ref,
                              yb_ref, h_ref)
        for g in range(SSD_GROUPS):
            fwd_group(g)
            bwd_group(g)


def _ssd_scan(xc, bt, dt, a_log, d_skip, h0=None):
    bsz, l, _ = xc.shape
    has_h0 = h0 is not None
    rows = SCAN_CHUNKS * SSD_CHUNK
    nc = l // rows
    fwd = lambda b, c: (b, c, 0)
    bwd = lambda b, c: (b, nc - 1 - c, 0)
    fwd_t = lambda b, c: (b, 0, c)
    bwd_t = lambda b, c: (b, 0, nc - 1 - c)
    st_spec = pl.BlockSpec((1, 2, SSD_STATE, SSD_INNER), lambda b, c: (b, 0, 0, 0))
    seg_tab, ind_ew = _scan_tables()
    return pl.pallas_call(
        functools.partial(_ssd_scan_kernel, has_h0=has_h0),
        grid=(bsz, nc),
        in_specs=[
            pl.BlockSpec((1, rows, SSD_INNER + SSD_GN), fwd),
            pl.BlockSpec((1, rows, SSD_INNER + SSD_GN), bwd),
            pl.BlockSpec((1, SSD_GN, rows), fwd_t),
            pl.BlockSpec((1, SSD_GN, rows), bwd_t),
            pl.BlockSpec((1, rows, LANES), fwd),
            pl.BlockSpec((1, rows, LANES), bwd),
            _const_spec((2, SSD_HEADS)),
            _const_spec((1, SSD_INNER)),
            _const_spec(seg_tab.shape),
            _const_spec(ind_ew.shape),
        ] + ([st_spec] if has_h0 else []),
        out_specs=[
            pl.BlockSpec((1, rows, SSD_INNER), fwd),
            pl.BlockSpec((1, rows, SSD_INNER), bwd),
            st_spec,
        ],
        out_shape=[
            jax.ShapeDtypeStruct((bsz, l, SSD_INNER), BF16),
            jax.ShapeDtypeStruct((bsz, l, SSD_INNER), BF16),
            jax.ShapeDtypeStruct((bsz, 2, SSD_STATE, SSD_INNER), F32),
        ],
        scratch_shapes=[pltpu.VMEM(seg_tab.shape, BF16)],
        compiler_params=_cparams("parallel", "arbitrary"),
        name="ssd_scan",
    )(xc, xc, bt, bt, dt, dt, a_log, d_skip, seg_tab, ind_ew, *([h0] if has_h0 else []))


def _ssd_out_kernel(h_ref, yf_ref, yb_ref, z_ref, mod_ref, g_ref, ng_ref, wo_ref, o_ref, v_scr):
    gw = SSD_INNER // SSD_GROUPS
    for g in range(SSD_GROUPS):
        cols = slice(g * gw, (g + 1) * gw)
        y = (yf_ref[0, :, cols].astype(F32) + yb_ref[0, :, cols].astype(F32)) \
            * _silu(z_ref[0, :, cols].astype(F32))
        y = y * lax.rsqrt(jnp.mean(y * y, axis=-1, keepdims=True) + EPS)
        v_scr[:, cols] = (y * ng_ref[:, cols]).astype(BF16)
    out = jnp.dot(v_scr[...], wo_ref[...], preferred_element_type=F32)
    o_ref[0] = h_ref[0] + mod_ref[0, 5:6, :] * _rms(out, g_ref[3:4, :])


def _ssd_out(h, yf, yb, z, mod, g, norm_g, w_out, *, tm, mod_row):
    bsz, l, _ = h.shape
    tok = lambda w: pl.BlockSpec((1, tm, w), lambda b, t: (b, t, 0))
    return pl.pallas_call(
        _ssd_out_kernel,
        grid=(bsz, l // tm),
        in_specs=[
            tok(D_MODEL), tok(SSD_INNER), tok(SSD_INNER), tok(SSD_INNER),
            pl.BlockSpec((1, N_MOD, D_MODEL), lambda b, t: (mod_row(b), 0, 0)),
            _const_spec((6, D_MODEL)),
            _const_spec((1, SSD_INNER)),
            _const_spec((SSD_INNER, D_MODEL)),
        ],
        out_specs=tok(D_MODEL),
        out_shape=jax.ShapeDtypeStruct(h.shape, F32),
        scratch_shapes=[pltpu.VMEM((tm, SSD_INNER), BF16)],
        compiler_params=_cparams("parallel", "parallel"),
        name="ssd_out",
    )(h, yf, yb, z, mod, g, norm_g.reshape(1, -1), w_out)


def _gmlp_kernel(h_ref, mod_ref, g_ref, wu_ref, wv_ref, vg_ref, vb_ref, ws_ref, bs_ref, wo_ref, o_ref,
                 p_scr, *, tm):
    rb = min(tm, ROW_BLOCK)
    for r in range(tm // rb):
        blk = slice(r * rb, (r + 1) * rb)
        h = h_ref[0, blk, :]
        u = (_rms(h, g_ref[2:3, :]) * (1.0 + mod_ref[0, 4:5, :]) + mod_ref[0, 3:4, :]).astype(BF16)
        gv = _gelu_tanh(jnp.dot(u, wv_ref[...], preferred_element_type=F32))
        mu = jnp.mean(gv, axis=-1, keepdims=True)
        var = jnp.mean(jnp.square(gv - mu), axis=-1, keepdims=True)
        gvn = ((gv - mu) * lax.rsqrt(var + EPS) * vg_ref[...] + vb_ref[...]).astype(BF16)
        for g in range(GM_GROUPS):
            cols = slice(g * GM_GROUP_DIM, (g + 1) * GM_GROUP_DIM)
            gu = _gelu_tanh(jnp.dot(u, wu_ref[:, cols], preferred_element_type=F32))
            for c in range(rb // GM_CHUNK):
                rows = slice(c * GM_CHUNK, (c + 1) * GM_CHUNK)
                s = jnp.dot(ws_ref[g], gvn[rows, cols], preferred_element_type=F32) + bs_ref[:, cols]
                p_scr[r * rb + c * GM_CHUNK:r * rb + (c + 1) * GM_CHUNK, cols] = (gu[rows, :] * s).astype(BF16)
        y = jnp.dot(p_scr[blk, :], wo_ref[...], preferred_element_type=F32)
        o_ref[0, blk, :] = h + mod_ref[0, 5:6, :] * _rms(y, g_ref[3:4, :])


def _gmlp(h, mod, g, w_in, v_g, v_b, w_s, b_s_full, w_out, *, tm, mod_row):
    bsz, l, _ = h.shape
    resident = dict(pipeline_mode=pl.Buffered(1))
    return pl.pallas_call(
        functools.partial(_gmlp_kernel, tm=tm),
        grid=(bsz, l // tm),
        in_specs=[
            pl.BlockSpec((1, tm, D_MODEL), lambda b, t: (b, t, 0)),
            pl.BlockSpec((1, N_MOD, D_MODEL), lambda b, t: (mod_row(b), 0, 0)),
            _const_spec((6, D_MODEL)),
            pl.BlockSpec((D_MODEL, GM_INNER), lambda b, t: (0, 0), **resident),
            pl.BlockSpec((D_MODEL, GM_INNER), lambda b, t: (0, 1), **resident),
            _const_spec((1, GM_INNER)),
            _const_spec((1, GM_INNER)),
            _const_spec((GM_GROUPS, GM_CHUNK, GM_CHUNK)),
            _const_spec((GM_CHUNK, GM_INNER)),
            pl.BlockSpec((GM_INNER, D_MODEL), lambda b, t: (0, 0), **resident),
        ],
        out_specs=pl.BlockSpec((1, tm, D_MODEL), lambda b, t: (b, t, 0)),
        out_shape=jax.ShapeDtypeStruct(h.shape, F32),
        scratch_shapes=[pltpu.VMEM((tm, GM_INNER), BF16)],
        compiler_params=_cparams("parallel", "parallel"),
        name="gmlp",
    )(h, mod, g, w_in, w_in, v_g.reshape(1, -1), v_b.reshape(1, -1), w_s, b_s_full, w_out)


def kernel(x, c, ctx, c_ctx, ada_w, ada_b, norm_g, ffn_w_in, ffn_w_out, ssd_w_in, ssd_conv_w, ssd_conv_b,
           ssd_dt_bias, ssd_A_log, ssd_D, ssd_norm_g, ssd_w_out, gm_w_in, gm_v_g, gm_v_b, gm_w_s, gm_b_s,
           gm_w_out):
    bsz, seq, _ = x.shape
    ctx_len = ctx.shape[1]
    cvec = jnp.zeros((MOD_ROWS, D_MODEL), F32).at[:bsz].set(c).at[bsz].set(c_ctx)
    mod_all = _ada_mod(cvec, ada_w, ada_b).reshape(DEPTH, MOD_ROWS, N_MOD, D_MODEL)

    x_row = lambda b: b
    ctx_row = lambda b: bsz
    tm_x = 512
    tm_ffn = 1024
    tm_c = ctx_len
    ffn_in = ffn_w_in.astype(BF16)
    ffn_out = ffn_w_out.astype(BF16)

    for i in range(DEPTH):
        use_ssd = (i % 2) == 0
        j = i // 2
        last = i == DEPTH - 1
        ctx_needed = (not last) or use_ssd
        ctx_full = not last
        mod = mod_all[i]
        g = norm_g[i]

        x = _ffn(x, mod, g, ffn_in, ffn_out, layer=i, k=0, s=0, tm=tm_ffn, mod_row=x_row)
        if ctx_needed:
            ctx = _ffn(ctx, mod, g, ffn_in, ffn_out, layer=i, k=0, s=0, tm=tm_c, mod_row=ctx_row)

        if use_ssd:
            w_in = ssd_w_in[j].astype(BF16)
            w_out = ssd_w_out[j].astype(BF16)
            d_skip = jnp.repeat(ssd_D[j], SSD_HEAD_DIM).reshape(1, SSD_INNER)
            dtb = ssd_dt_bias[j].reshape(-1)

            def branch(h, h0, tm, mod_row):
                z, xc, bt, dt = _ssd_in(h, mod, g, w_in, ssd_conv_w[j], ssd_conv_b[j], dtb,
                                        tm=tm, mod_row=mod_row)
                yf, yb, hT = _ssd_scan(xc, bt, dt, ssd_A_log[j], d_skip, h0)
                return yf, yb, z, hT

            cyf, cyb, cz, h_ctx = branch(ctx, None, tm_c, ctx_row)
            xyf, xyb, xz, _ = branch(x, h_ctx, tm_x, x_row)
            x = _ssd_out(x, xyf, xyb, xz, mod, g, ssd_norm_g[j], w_out, tm=tm_x, mod_row=x_row)
            if ctx_full:
                ctx = _ssd_out(ctx, cyf, cyb, cz, mod, g, ssd_norm_g[j], w_out, tm=tm_c, mod_row=ctx_row)
        else:
            gw_in = gm_w_in[j].astype(BF16)
            gw_out = gm_w_out[j].astype(BF16)
            ws = gm_w_s[j].astype(BF16)
            bs_full = jnp.repeat(gm_b_s[j].T, GM_GROUP_DIM, axis=1)
            x = _gmlp(x, mod, g, gw_in, gm_v_g[j], gm_v_b[j], ws, bs_full, gw_out, tm=tm_ffn, mod_row=x_row)
            if ctx_full:
                ctx = _gmlp(ctx, mod, g, gw_in, gm_v_g[j], gm_v_b[j], ws, bs_full, gw_out,
                            tm=tm_c, mod_row=ctx_row)

        x = _ffn(x, mod, g, ffn_in, ffn_out, layer=i, k=1, s=2, tm=tm_ffn, mod_row=x_row)
        if ctx_full:
            ctx = _ffn(ctx, mod, g, ffn_in, ffn_out, layer=i, k=1, s=2, tm=tm_c, mod_row=ctx_row)
    return x
```

```python
import functools
import math

import jax
import jax.numpy as jnp
from jax import lax
from jax.experimental import pallas as pl
from jax.experimental.pallas import tpu as pltpu

F32 = jnp.float32
BF16 = jnp.bfloat16

D_MODEL = 1024
DEPTH = 2
N_MOD = 9
MACARON_W = 0.5
EPS = 1e-6
FFN_DIM = 2816

SSD_INNER = 2048
SSD_HEAD_DIM = 64
SSD_HEADS = 32
SSD_GROUPS = 8
SSD_HPG = 4
SSD_STATE = 128
SSD_CONV = 5
SSD_CHUNK = 128
SSD_GN = SSD_GROUPS * SSD_STATE
SSD_CONV_DIM = SSD_INNER + 2 * SSD_GN

GM_CHUNK = 128
GM_INNER = 2048
GM_GROUPS = 8
GM_GROUP_DIM = 256

MXU_TILE = 256
LANES = 128
SUBLANES = 8
BF16_ROWS = 16
SCAN_CHUNKS = 4
ROW_BLOCK = 512
HALO = 16
VMEM_LIMIT = 56 * 1024 * 1024
MOD_ROWS = 16
LOG2E = math.log2(math.e)
NEG_BIG = -1e30


def _cparams(*sem):
    return pltpu.CompilerParams(dimension_semantics=sem, vmem_limit_bytes=VMEM_LIMIT)


def _rms(x, g):
    return x * lax.rsqrt(jnp.mean(x * x, axis=-1, keepdims=True) + EPS) * g


def _silu(x):
    return x * jax.nn.sigmoid(x)


def _gelu_tanh(x):
    k = -2.0 * LOG2E * math.sqrt(2.0 / math.pi)
    t = x * (k + (k * 0.044715) * (x * x))
    return x / (1.0 + jnp.exp2(t))


def _const_spec(shape):
    nd = len(shape)
    return pl.BlockSpec(shape, lambda *_: (0,) * nd)


def _ada_kernel(c_ref, w_ref, b_ref, o_ref):
    cs = _silu(c_ref[...]).astype(BF16)
    o_ref[...] = jnp.dot(cs, w_ref[...].astype(BF16), preferred_element_type=F32) + b_ref[...]


def _ada_mod(cvec, ada_w, ada_b):
    nd = N_MOD * D_MODEL
    bn = 1024
    return pl.pallas_call(
        _ada_kernel,
        grid=(DEPTH, nd // bn),
        in_specs=[
            pl.BlockSpec((MOD_ROWS, D_MODEL), lambda i, j: (0, 0)),
            pl.BlockSpec((None, D_MODEL, bn), lambda i, j: (i, 0, j)),
            pl.BlockSpec((None, 1, bn), lambda i, j: (i, 0, j)),
        ],
        out_specs=pl.BlockSpec((None, MOD_ROWS, bn), lambda i, j: (i, 0, j)),
        out_shape=jax.ShapeDtypeStruct((DEPTH, MOD_ROWS, nd), F32),
        compiler_params=_cparams("parallel", "parallel"),
        name="ada_mod",
    )(cvec, ada_w, ada_b.reshape(DEPTH, 1, nd))


def _ffn_kernel(h_ref, mod_ref, g_ref, wg_ref, wu_ref, wo_ref, o_ref, a_scr, *, s, tm):
    shift = mod_ref[0, 3 * s:3 * s + 1, :]
    scale = mod_ref[0, 3 * s + 1:3 * s + 2, :]
    gate = mod_ref[0, 3 * s + 2:3 * s + 3, :]
    rb = min(tm, ROW_BLOCK)
    for r in range(tm // rb):
        rows = slice(r * rb, (r + 1) * rb)
        h = h_ref[0, rows, :]
        u = (_rms(h, g_ref[2 * s:2 * s + 1, :]) * (1.0 + scale) + shift).astype(BF16)
        for c in range(FFN_DIM // MXU_TILE):
            cols = slice(c * MXU_TILE, (c + 1) * MXU_TILE)
            hg = jnp.dot(u, wg_ref[:, cols], preferred_element_type=F32)
            hu = jnp.dot(u, wu_ref[:, cols], preferred_element_type=F32)
            a_scr[rows, cols] = (_silu(hg) * hu).astype(BF16)
        y = jnp.dot(a_scr[rows, :], wo_ref[...], preferred_element_type=F32)
        o_ref[0, rows, :] = h + (MACARON_W * gate) * _rms(y, g_ref[2 * s + 1:2 * s + 2, :])


def _ffn(h, mod, g, w_in, w_out, *, layer, k, s, tm, mod_row):
    bsz, l, _ = h.shape
    resident = dict(pipeline_mode=pl.Buffered(1))
    return pl.pallas_call(
        functools.partial(_ffn_kernel, s=s, tm=tm),
        grid=(bsz, l // tm),
        in_specs=[
            pl.BlockSpec((1, tm, D_MODEL), lambda b, t: (b, t, 0)),
            pl.BlockSpec((1, N_MOD, D_MODEL), lambda b, t: (mod_row(b), 0, 0)),
            _const_spec((6, D_MODEL)),
            pl.BlockSpec((None, None, D_MODEL, FFN_DIM), lambda b, t: (layer, k, 0, 0), **resident),
            pl.BlockSpec((None, None, D_MODEL, FFN_DIM), lambda b, t: (layer, k, 0, 1), **resident),
            pl.BlockSpec((None, None, FFN_DIM, D_MODEL), lambda b, t: (layer, k, 0, 0), **resident),
        ],
        out_specs=pl.BlockSpec((1, tm, D_MODEL), lambda b, t: (b, t, 0)),
        out_shape=jax.ShapeDtypeStruct(h.shape, F32),
        scratch_shapes=[pltpu.VMEM((tm, FFN_DIM), BF16)],
        compiler_params=_cparams("parallel", "parallel"),
        name="ffn",
    )(h, mod, g, w_in, w_in, w_out)


def _ssd_in_kernel(xm_ref, xp_ref, xn_ref, mod_ref, g_ref, wz_ref, wxa_ref, wxb_ref, wdt_ref, cw_ref, cb_ref,
                   dtb_ref, z_ref, xc_ref, bt_ref, dt_ref, u_scr, p_scr, o_scr, *, tm):
    t = pl.program_id(1)
    nt = pl.num_programs(1)
    shift = mod_ref[0, 3:4, :]
    scale = mod_ref[0, 4:5, :]
    g_pre = g_ref[2:3, :]

    def prep(h):
        return _rms(h, g_pre) * (1.0 + scale) + shift

    u_scr[0:HALO, :] = jnp.where(t == 0, 0.0, prep(xp_ref[0])).astype(BF16)
    u_scr[HALO:HALO + tm, :] = prep(xm_ref[0]).astype(BF16)
    u_scr[HALO + tm:, :] = jnp.where(t == nt - 1, 0.0, prep(xn_ref[0])).astype(BF16)
    um = u_scr[HALO:HALO + tm, :]
    dt_raw = jnp.dot(um, wdt_ref[...], preferred_element_type=F32) + dtb_ref[...]
    dt_ref[0] = jnp.maximum(dt_raw, 0.0) + jnp.log1p(jnp.exp(-jnp.abs(dt_raw)))

    nblk = SSD_CONV_DIM // LANES
    npair = nblk // 2
    pairs_a = SSD_INNER // MXU_TILE
    groups8 = tm // (SUBLANES * SUBLANES)

    def project(pair):
        if pair < pairs_a:
            w = wxa_ref[:, pair * MXU_TILE:(pair + 1) * MXU_TILE]
        else:
            w = wxb_ref[:, (pair - pairs_a) * MXU_TILE:(pair - pairs_a + 1) * MXU_TILE]
        p = jnp.dot(u_scr[...], w, preferred_element_type=F32)
        p_scr[2 * pair] = p[:, :LANES]
        p_scr[2 * pair + 1] = p[:, LANES:]

    def conv_block(blk):
        cols = slice(blk * LANES, (blk + 1) * LANES)
        wk = [jnp.broadcast_to(cw_ref[k:k + 1, cols], (SUBLANES, LANES)) for k in range(SSD_CONV)]
        bias = jnp.broadcast_to(cb_ref[:, cols], (SUBLANES, LANES))
        slot = blk % 2
        for m in range(groups8):
            base = HALO - SSD_CONV // 2 + SUBLANES * SUBLANES * m
            slabs = [p_scr[blk, pl.ds(base + r, SUBLANES, stride=SUBLANES), :]
                     for r in range(SSD_CONV - 1)]
            for s in range(SUBLANES):
                slabs.append(p_scr[blk, pl.ds(base + s + SSD_CONV - 1, SUBLANES, stride=SUBLANES), :])
                acc = bias
                for k in range(SSD_CONV):
                    acc = acc + wk[k] * slabs[s + k]
                o_scr[slot, pl.ds(SUBLANES * SUBLANES * m + s, SUBLANES, stride=SUBLANES), :] = _silu(acc)
        val = o_scr[slot]
        nx = SSD_INNER // LANES
        nb = SSD_GN // LANES
        if blk < nx:
            xc_ref[0, :, cols] = val.astype(BF16)
        elif blk < nx + nb:
            rows = slice((blk - nx) * LANES, (blk - nx + 1) * LANES)
            bt_ref[0, rows, :] = val.T.astype(BF16)
        else:
            oc = slice((blk - nb) * LANES, (blk - nb + 1) * LANES)
            xc_ref[0, :, oc] = val.astype(BF16)

    project(0)
    z_ref[0] = jnp.dot(um, wz_ref[...], preferred_element_type=F32).astype(BF16)
    for pair in range(npair):
        conv_block(2 * pair)
        if pair + 1 < npair:
            project(pair + 1)
        conv_block(2 * pair + 1)


def _ssd_in(h, mod, g, w_in, conv_w, conv_b, dt_bias, *, tm, mod_row):
    bsz, l, _ = h.shape
    nh = tm // HALO
    last = l // HALO - 1
    dt_pad = LANES - 2 * SSD_HEADS
    wdt = jnp.pad(w_in[:, SSD_INNER + SSD_CONV_DIM:], ((0, 0), (0, dt_pad)))
    dt_bias = jnp.pad(dt_bias, (0, dt_pad))
    wblk = lambda j: pl.BlockSpec((D_MODEL, SSD_INNER), lambda b, t: (0, j))
    return pl.pallas_call(
        functools.partial(_ssd_in_kernel, tm=tm),
        grid=(bsz, l // tm),
        in_specs=[
            pl.BlockSpec((1, tm, D_MODEL), lambda b, t: (b, t, 0)),
            pl.BlockSpec((1, HALO, D_MODEL), lambda b, t: (b, jnp.maximum(t * nh - 1, 0), 0)),
            pl.BlockSpec((1, HALO, D_MODEL), lambda b, t: (b, jnp.minimum((t + 1) * nh, last), 0)),
            pl.BlockSpec((1, N_MOD, D_MODEL), lambda b, t: (mod_row(b), 0, 0)),
            _const_spec((6, D_MODEL)),
            wblk(0), wblk(1), wblk(2),
            _const_spec((D_MODEL, LANES)),
            _const_spec((SSD_CONV, SSD_CONV_DIM)),
            _const_spec((1, SSD_CONV_DIM)),
            _const_spec((1, LANES)),
        ],
        out_specs=[
            pl.BlockSpec((1, tm, SSD_INNER), lambda b, t: (b, t, 0)),
            pl.BlockSpec((1, tm, SSD_INNER + SSD_GN), lambda b, t: (b, t, 0)),
            pl.BlockSpec((1, SSD_GN, tm), lambda b, t: (b, 0, t)),
            pl.BlockSpec((1, tm, LANES), lambda b, t: (b, t, 0)),
        ],
        out_shape=[
            jax.ShapeDtypeStruct((bsz, l, SSD_INNER), BF16),
            jax.ShapeDtypeStruct((bsz, l, SSD_INNER + SSD_GN), BF16),
            jax.ShapeDtypeStruct((bsz, SSD_GN, l), BF16),
            jax.ShapeDtypeStruct((bsz, l, LANES), F32),
        ],
        scratch_shapes=[
            pltpu.VMEM((tm + 2 * HALO, D_MODEL), BF16),
            pltpu.VMEM((SSD_CONV_DIM // LANES, tm + 2 * HALO, LANES), F32),
            pltpu.VMEM((2, tm, LANES), F32),
        ],
        compiler_params=_cparams("parallel", "parallel"),
        name="ssd_in",
    )(h, h, h, mod, g, w_in, w_in, w_in, wdt, conv_w, conv_b.reshape(1, -1), dt_bias.reshape(1, -1))


def _split_bf16(v, n):
    parts = []
    for _ in range(n):
        p = v.astype(BF16).astype(F32)
        parts.append(p)
        v = v - p
    return parts


def _scan_tables():
    q = SSD_CHUNK
    r = jnp.arange(LANES)[:, None]
    ca = jnp.arange(SSD_HEADS * q)[None, :]
    ind_a = (((r % SSD_HEADS) == (ca // q)) & (r < 3 * SSD_HEADS)).astype(F32)
    j = ca % q
    mask_f = jnp.where(j > r, NEG_BIG, 0.0)
    mask_b = jnp.where(j < r, NEG_BIG, 0.0)
    seg_tab = jnp.stack([jnp.concatenate([ind_a, mask_f], axis=0),
                         jnp.concatenate([ind_a, mask_b], axis=0)]).astype(BF16)
    ce = jnp.arange(2 * SSD_INNER)[None, :]
    ind_ew = ((r % SSD_HEADS) == ((ce % SSD_INNER) // SSD_HEAD_DIM)) \
        & ((r // (2 * SSD_HEADS)) == (ce // SSD_INNER))
    return seg_tab, ind_ew.astype(BF16)


def _scan_dir(d, ro, xc_ref, bt_ref, dt_ref, alog_ref, dskip_ref, tab_scr, indew_ref, y_ref, h_ref):
    q = SSD_CHUNK
    nh = SSD_HEADS
    rws = slice(ro, ro + q)
    dt = dt_ref[0, rws, nh * d:nh * (d + 1)]
    a_neg = -jnp.exp(alog_ref[d:d + 1, :])
    row = lax.broadcasted_iota(jnp.int32, (q, q), 0)
    col = lax.broadcasted_iota(jnp.int32, (q, q), 1)
    tri = (row >= col) if d == 0 else (row <= col)
    pad_a = jnp.zeros((q, LANES - 3 * nh), F32)
    cs = jnp.dot(tri.astype(BF16),
                 jnp.concatenate(_split_bf16(dt * a_neg, 3) + [pad_a], axis=1).astype(BF16),
                 preferred_element_type=F32)
    a = cs[:, 0:nh] + cs[:, nh:2 * nh] + cs[:, 2 * nh:3 * nh]
    a_end = a[q - 1:q, :] if d == 0 else a[0:1, :]
    e_in = jnp.exp(a)
    w_end = dt * jnp.exp(a_end - a)
    r_nat = LOG2E * (jnp.maximum(jnp.log(dt), NEG_BIG) - a)
    r_t = jnp.concatenate([r_nat, jnp.zeros((q, LANES - nh), F32)], axis=1).T[0:nh]
    r_parts = _split_bf16(r_t, 3)
    prow = lax.broadcasted_iota(jnp.int32, (BF16_ROWS, q), 0)
    for hd in range(nh):
        tile = jnp.where(prow == 0, r_parts[0][hd:hd + 1, :],
                         jnp.where(prow == 1, r_parts[1][hd:hd + 1, :],
                                   jnp.where(prow == 2, r_parts[2][hd:hd + 1, :], 0.0)))
        tab_scr[d, 3 * nh:3 * nh + BF16_ROWS, q * hd:q * (hd + 1)] = tile.astype(BF16)
    ones3 = (lax.broadcasted_iota(jnp.int32, (q, LANES - 3 * nh), 1) < 3).astype(F32)
    eye = (row == col).astype(F32)
    seg_lhs = jnp.concatenate(_split_bf16(LOG2E * a, 3) + [ones3, eye], axis=1).astype(BF16)
    ew_parts = jnp.concatenate(_split_bf16(e_in, 2) + _split_bf16(w_end, 2), axis=1).astype(BF16)
    low_half = lax.broadcasted_iota(jnp.int32, (q, 2 * SSD_HEAD_DIM), 1) < SSD_HEAD_DIM

    gw = SSD_HPG * SSD_HEAD_DIM

    def group(g):
        gs = slice(gw * g, gw * (g + 1))
        c_g = xc_ref[0, rws, SSD_INNER + SSD_STATE * g:SSD_INNER + SSD_STATE * (g + 1)]
        bt_g = bt_ref[0, SSD_STATE * g:SSD_STATE * (g + 1), rws]
        x_g = xc_ref[0, rws, gs]
        ht_g = h_ref[0, d, :, gs]
        cb = jnp.dot(c_g, bt_g, preferred_element_type=F32)
        seg = jnp.dot(seg_lhs, tab_scr[d, :, SSD_HPG * q * g:SSD_HPG * q * (g + 1)],
                      preferred_element_type=F32)
        e_rep = jnp.dot(ew_parts, indew_ref[:, gs], preferred_element_type=F32)
        w_rep = jnp.dot(ew_parts, indew_ref[:, SSD_INNER + gw * g:SSD_INNER + gw * (g + 1)],
                        preferred_element_type=F32)
        y_g = jnp.dot(c_g, ht_g.astype(BF16), preferred_element_type=F32) * e_rep
        if d == 0:
            y_g = y_g + dskip_ref[:, gs] * x_g.astype(F32)
        ms = [(cb * jnp.exp2(seg[:, q * k:q * (k + 1)])).astype(BF16) for k in range(SSD_HPG)]
        y_pairs = []
        for pr in range(SSD_HPG // 2):
            x_pair = x_g[:, 2 * SSD_HEAD_DIM * pr:2 * SSD_HEAD_DIM * (pr + 1)]
            zero = jnp.zeros_like(x_pair)
            x_bd = jnp.concatenate([jnp.where(low_half, x_pair, zero),
                                    jnp.where(low_half, zero, x_pair)], axis=0)
            m_pair = jnp.concatenate([ms[2 * pr], ms[2 * pr + 1]], axis=1)
            y_pairs.append(jnp.dot(m_pair, x_bd, preferred_element_type=F32))
        y_ref[0, rws, gs] = (y_g + jnp.concatenate(y_pairs, axis=1)).astype(BF16)
        xw = (x_g.astype(F32) * w_rep).astype(BF16)
        s_g = jnp.dot(bt_g, xw, preferred_element_type=F32)
        dec_row = e_rep[q - 1:q, :] if d == 0 else e_rep[0:1, :]
        h_ref[0, d, :, gs] = ht_g * dec_row + s_g

    return group


def _ssd_scan_kernel(xf_ref, xb_ref, btf_ref, btb_ref, dtf_ref, dtb_ref, alog_ref, dskip_ref,
                     tab_ref, indew_ref, *rest, has_h0, chunks):
    if has_h0:
        h0_ref, yf_ref, yb_ref, h_ref, tab_scr = rest
    else:
        yf_ref, yb_ref, h_ref, tab_scr = rest

    @pl.when(pl.program_id(1) == 0)
    def _():
        h_ref[...] = h0_ref[...] if has_h0 else jnp.zeros(h_ref.shape, F32)
        tab_scr[...] = tab_ref[...]

    for sub in range(chunks):
        ro_f = sub * SSD_CHUNK
        ro_b = (chunks - 1 - sub) * SSD_CHUNK
        fwd_group = _scan_dir(0, ro_f, xf_ref, btf_ref, dtf_ref, alog_ref, dskip_ref, tab_scr, indew_ref,
                              yf_ref, h_ref)
        bwd_group = _scan_dir(1, ro_b, xb_ref, btb_ref, dtb_ref, alog_ref, dskip_ref, tab_scr, indew_ref,
                              yb_ref, h_ref)
        for g in range(SSD_GROUPS):
            fwd_group(g)
            bwd_group(g)


def _ssd_scan(xc, bt, dt, a_log, d_skip, h0=None):
    bsz, l, _ = xc.shape
    has_h0 = h0 is not None
    chunks = min(SCAN_CHUNKS, l // SSD_CHUNK)
    rows = chunks * SSD_CHUNK
    nc = l // rows
    fwd = lambda b, c: (b, c, 0)
    bwd = lambda b, c: (b, nc - 1 - c, 0)
    fwd_t = lambda b, c: (b, 0, c)
    bwd_t = lambda b, c: (b, 0, nc - 1 - c)
    st_spec = pl.BlockSpec((1, 2, SSD_STATE, SSD_INNER), lambda b, c: (b, 0, 0, 0))
    seg_tab, ind_ew = _scan_tables()
    return pl.pallas_call(
        functools.partial(_ssd_scan_kernel, has_h0=has_h0, chunks=chunks),
        grid=(bsz, nc),
        in_specs=[
            pl.BlockSpec((1, rows, SSD_INNER + SSD_GN), fwd),
            pl.BlockSpec((1, rows, SSD_INNER + SSD_GN), bwd),
            pl.BlockSpec((1, SSD_GN, rows), fwd_t),
            pl.BlockSpec((1, SSD_GN, rows), bwd_t),
            pl.BlockSpec((1, rows, LANES), fwd),
            pl.BlockSpec((1, rows, LANES), bwd),
            _const_spec((2, SSD_HEADS)),
            _const_spec((1, SSD_INNER)),
            _const_spec(seg_tab.shape),
            _const_spec(ind_ew.shape),
        ] + ([st_spec] if has_h0 else []),
        out_specs=[
            pl.BlockSpec((1, rows, SSD_INNER), fwd),
            pl.BlockSpec((1, rows, SSD_INNER), bwd),
            st_spec,
        ],
        out_shape=[
            jax.ShapeDtypeStruct((bsz, l, SSD_INNER), BF16),
            jax.ShapeDtypeStruct((bsz, l, SSD_INNER), BF16),
            jax.ShapeDtypeStruct((bsz, 2, SSD_STATE, SSD_INNER), F32),
        ],
        scratch_shapes=[pltpu.VMEM(seg_tab.shape, BF16)],
        compiler_params=_cparams("parallel", "arbitrary"),
        name="ssd_scan",
    )(xc, xc, bt, bt, dt, dt, a_log, d_skip, seg_tab, ind_ew, *([h0] if has_h0 else []))


def _ssd_out_kernel(h_ref, yf_ref, yb_ref, z_ref, mod_ref, g_ref, ng_ref, wo_ref, o_ref, v_scr):
    gw = SSD_INNER // SSD_GROUPS
    for g in range(SSD_GROUPS):
        cols = slice(g * gw, (g + 1) * gw)
        y = (yf_ref[0, :, cols].astype(F32) + yb_ref[0, :, cols].astype(F32)) \
            * _silu(z_ref[0, :, cols].astype(F32))
        y = y * lax.rsqrt(jnp.mean(y * y, axis=-1, keepdims=True) + EPS)
        v_scr[:, cols] = (y * ng_ref[:, cols]).astype(BF16)
    out = jnp.dot(v_scr[...], wo_ref[...], preferred_element_type=F32)
    o_ref[0] = h_ref[0] + mod_ref[0, 5:6, :] * _rms(out, g_ref[3:4, :])


def _ssd_out(h, yf, yb, z, mod, g, norm_g, w_out, *, tm, mod_row):
    bsz, l, _ = h.shape
    tok = lambda w: pl.BlockSpec((1, tm, w), lambda b, t: (b, t, 0))
    return pl.pallas_call(
        _ssd_out_kernel,
        grid=(bsz, l // tm),
        in_specs=[
            tok(D_MODEL), tok(SSD_INNER), tok(SSD_INNER), tok(SSD_INNER),
            pl.BlockSpec((1, N_MOD, D_MODEL), lambda b, t: (mod_row(b), 0, 0)),
            _const_spec((6, D_MODEL)),
            _const_spec((1, SSD_INNER)),
            _const_spec((SSD_INNER, D_MODEL)),
        ],
        out_specs=tok(D_MODEL),
        out_shape=jax.ShapeDtypeStruct(h.shape, F32),
        scratch_shapes=[pltpu.VMEM((tm, SSD_INNER), BF16)],
        compiler_params=_cparams("parallel", "parallel"),
        name="ssd_out",
    )(h, yf, yb, z, mod, g, norm_g.reshape(1, -1), w_out)


def _gmlp_kernel(h_ref, mod_ref, g_ref, wu_ref, wv_ref, vg_ref, vb_ref, ws_ref, bs_ref, wo_ref, o_ref,
                 p_scr, *, tm):
    rb = min(tm, ROW_BLOCK)
    for r in range(tm // rb):
        blk = slice(r * rb, (r + 1) * rb)
        h = h_ref[0, blk, :]
        u = (_rms(h, g_ref[2:3, :]) * (1.0 + mod_ref[0, 4:5, :]) + mod_ref[0, 3:4, :]).astype(BF16)
        gv = _gelu_tanh(jnp.dot(u, wv_ref[...], preferred_element_type=F32))
        mu = jnp.mean(gv, axis=-1, keepdims=True)
        var = jnp.mean(jnp.square(gv - mu), axis=-1, keepdims=True)
        gvn = ((gv - mu) * lax.rsqrt(var + EPS) * vg_ref[...] + vb_ref[...]).astype(BF16)
        for g in range(GM_GROUPS):
            cols = slice(g * GM_GROUP_DIM, (g + 1) * GM_GROUP_DIM)
            gu = _gelu_tanh(jnp.dot(u, wu_ref[:, cols], preferred_element_type=F32))
            for c in range(rb // GM_CHUNK):
                rows = slice(c * GM_CHUNK, (c + 1) * GM_CHUNK)
                s = jnp.dot(ws_ref[g], gvn[rows, cols], preferred_element_type=F32) + bs_ref[:, cols]
                p_scr[r * rb + c * GM_CHUNK:r * rb + (c + 1) * GM_CHUNK, cols] = (gu[rows, :] * s).astype(BF16)
        y = jnp.dot(p_scr[blk, :], wo_ref[...], preferred_element_type=F32)
        o_ref[0, blk, :] = h + mod_ref[0, 5:6, :] * _rms(y, g_ref[3:4, :])


def _gmlp(h, mod, g, w_in, v_g, v_b, w_s, b_s_full, w_out, *, tm, mod_row):
    bsz, l, _ = h.shape
    resident = dict(pipeline_mode=pl.Buffered(1))
    return pl.pallas_call(
        functools.partial(_gmlp_kernel, tm=tm),
        grid=(bsz, l // tm),
        in_specs=[
            pl.BlockSpec((1, tm, D_MODEL), lambda b, t: (b, t, 0)),
            pl.BlockSpec((1, N_MOD, D_MODEL), lambda b, t: (mod_row(b), 0, 0)),
            _const_spec((6, D_MODEL)),
            pl.BlockSpec((D_MODEL, GM_INNER), lambda b, t: (0, 0), **resident),
            pl.BlockSpec((D_MODEL, GM_INNER), lambda b, t: (0, 1), **resident),
            _const_spec((1, GM_INNER)),
            _const_spec((1, GM_INNER)),
            _const_spec((GM_GROUPS, GM_CHUNK, GM_CHUNK)),
            _const_spec((GM_CHUNK, GM_INNER)),
            pl.BlockSpec((GM_INNER, D_MODEL), lambda b, t: (0, 0), **resident),
        ],
        out_specs=pl.BlockSpec((1, tm, D_MODEL), lambda b, t: (b, t, 0)),
        out_shape=jax.ShapeDtypeStruct(h.shape, F32),
        scratch_shapes=[pltpu.VMEM((tm, GM_INNER), BF16)],
        compiler_params=_cparams("parallel", "parallel"),
        name="gmlp",
    )(h, mod, g, w_in, w_in, v_g.reshape(1, -1), v_b.reshape(1, -1), w_s, b_s_full, w_out)


def kernel(x, c, ctx, c_ctx, ada_w, ada_b, norm_g, ffn_w_in, ffn_w_out, ssd_w_in, ssd_conv_w, ssd_conv_b,
           ssd_dt_bias, ssd_A_log, ssd_D, ssd_norm_g, ssd_w_out, gm_w_in, gm_v_g, gm_v_b, gm_w_s, gm_b_s,
           gm_w_out):
    bsz, seq, _ = x.shape
    ctx_len = ctx.shape[1]
    cvec = jnp.zeros((MOD_ROWS, D_MODEL), F32).at[:bsz].set(c).at[bsz].set(c_ctx)
    mod_all = _ada_mod(cvec, ada_w, ada_b).reshape(DEPTH, MOD_ROWS, N_MOD, D_MODEL)

    x_row = lambda b: b
    ctx_row = lambda b: bsz
    tm_x = 512
    tm_ffn = 1024
    tm_c = ctx_len
    ffn_in = ffn_w_in.astype(BF16)
    ffn_out = ffn_w_out.astype(BF16)

    for i in range(DEPTH):
        use_ssd = (i % 2) == 0
        j = i // 2
        last = i == DEPTH - 1
        ctx_needed = (not last) or use_ssd
        ctx_full = not last
        mod = mod_all[i]
        g = norm_g[i]

        x = _ffn(x, mod, g, ffn_in, ffn_out, layer=i, k=0, s=0, tm=tm_ffn, mod_row=x_row)
        if ctx_needed:
            ctx = _ffn(ctx, mod, g, ffn_in, ffn_out, layer=i, k=0, s=0, tm=tm_c, mod_row=ctx_row)

        if use_ssd:
            w_in = ssd_w_in[j].astype(BF16)
            w_out = ssd_w_out[j].astype(BF16)
            d_skip = jnp.repeat(ssd_D[j], SSD_HEAD_DIM).reshape(1, SSD_INNER)
            dtb = ssd_dt_bias[j].reshape(-1)

            def branch(h, h0, tm, mod_row):
                z, xc, bt, dt = _ssd_in(h, mod, g, w_in, ssd_conv_w[j], ssd_conv_b[j], dtb,
                                        tm=tm, mod_row=mod_row)
                yf, yb, hT = _ssd_scan(xc, bt, dt, ssd_A_log[j], d_skip, h0)
                return yf, yb, z, hT

            cyf, cyb, cz, h_ctx = branch(ctx, None, tm_c, ctx_row)
            xyf, xyb, xz, _ = branch(x, h_ctx, tm_x, x_row)
            x = _ssd_out(x, xyf, xyb, xz, mod, g, ssd_norm_g[j], w_out, tm=tm_x, mod_row=x_row)
            if ctx_full:
                ctx = _ssd_out(ctx, cyf, cyb, cz, mod, g, ssd_norm_g[j], w_out, tm=tm_c, mod_row=ctx_row)
        else:
            gw_in = gm_w_in[j].astype(BF16)
            gw_out = gm_w_out[j].astype(BF16)
            ws = gm_w_s[j].astype(BF16)
            bs_full = jnp.repeat(gm_b_s[j].T, GM_GROUP_DIM, axis=1)
            x = _gmlp(x, mod, g, gw_in, gm_v_g[j], gm_v_b[j], ws, bs_full, gw_out, tm=tm_ffn, mod_row=x_row)
            if ctx_full:
                ctx = _gmlp(ctx, mod, g, gw_in, gm_v_g[j], gm_v_b[j], ws, bs_full, gw_out,
                            tm=tm_c, mod_row=ctx_row)

        x = _ffn(x, mod, g, ffn_in, ffn_out, layer=i, k=1, s=2, tm=tm_ffn, mod_row=x_row)
        if ctx_full:
            ctx = _ffn(ctx, mod, g, ffn_in, ffn_out, layer=i, k=1, s=2, tm=tm_c, mod_row=ctx_row)
    return x
```

```python
import functools
import math

import jax
import jax.numpy as jnp
from jax import lax
from jax.experimental import pallas as pl
from jax.experimental.pallas import tpu as pltpu

F32 = jnp.float32
BF16 = jnp.bfloat16

D_MODEL = 1024
DEPTH = 2
N_MOD = 9
MACARON_W = 0.5
EPS = 1e-6
FFN_DIM = 2816

SSD_INNER = 2048
SSD_HEAD_DIM = 64
SSD_HEADS = 32
SSD_GROUPS = 8
SSD_HPG = 4
SSD_STATE = 128
SSD_CONV = 5
SSD_CHUNK = 128
SSD_GN = SSD_GROUPS * SSD_STATE
SSD_CONV_DIM = SSD_INNER + 2 * SSD_GN

GM_CHUNK = 128
GM_INNER = 2048
GM_GROUPS = 8
GM_GROUP_DIM = 256

MXU_TILE = 256
LANES = 128
SUBLANES = 8
BF16_ROWS = 16
SCAN_CHUNKS = 4
ROW_BLOCK = 512
HALO = 16
VMEM_LIMIT = 56 * 1024 * 1024
MOD_ROWS = 16
LOG2E = math.log2(math.e)
NEG_BIG = -1e30


def _cparams(*sem):
    return pltpu.CompilerParams(dimension_semantics=sem, vmem_limit_bytes=VMEM_LIMIT)


def _rms(x, g):
    return x * lax.rsqrt(jnp.mean(x * x, axis=-1, keepdims=True) + EPS) * g


def _silu(x):
    return x * jax.nn.sigmoid(x)


def _gelu_tanh(x):
    k = -2.0 * LOG2E * math.sqrt(2.0 / math.pi)
    t = x * (k + (k * 0.044715) * (x * x))
    return x / (1.0 + jnp.exp2(t))


def _const_spec(shape):
    nd = len(shape)
    return pl.BlockSpec(shape, lambda *_: (0,) * nd)


def _ada_kernel(c_ref, w_ref, b_ref, o_ref):
    cs = _silu(c_ref[...]).astype(BF16)
    o_ref[...] = jnp.dot(cs, w_ref[...].astype(BF16), preferred_element_type=F32) + b_ref[...]


def _ada_mod(cvec, ada_w, ada_b):
    nd = N_MOD * D_MODEL
    bn = 1024
    return pl.pallas_call(
        _ada_kernel,
        grid=(DEPTH, nd // bn),
        in_specs=[
            pl.BlockSpec((MOD_ROWS, D_MODEL), lambda i, j: (0, 0)),
            pl.BlockSpec((None, D_MODEL, bn), lambda i, j: (i, 0, j)),
            pl.BlockSpec((None, 1, bn), lambda i, j: (i, 0, j)),
        ],
        out_specs=pl.BlockSpec((None, MOD_ROWS, bn), lambda i, j: (i, 0, j)),
        out_shape=jax.ShapeDtypeStruct((DEPTH, MOD_ROWS, nd), F32),
        compiler_params=_cparams("parallel", "parallel"),
        name="ada_mod",
    )(cvec, ada_w, ada_b.reshape(DEPTH, 1, nd))


def _ffn_kernel(h_ref, mod_ref, g_ref, wg_ref, wu_ref, wo_ref, o_ref, a_scr, *, s, tm):
    shift = mod_ref[0, 3 * s:3 * s + 1, :]
    scale = mod_ref[0, 3 * s + 1:3 * s + 2, :]
    gate = mod_ref[0, 3 * s + 2:3 * s + 3, :]
    rb = min(tm, ROW_BLOCK)
    for r in range(tm // rb):
        rows = slice(r * rb, (r + 1) * rb)
        h = h_ref[0, rows, :]
        u = (_rms(h, g_ref[2 * s:2 * s + 1, :]) * (1.0 + scale) + shift).astype(BF16)
        for c in range(FFN_DIM // MXU_TILE):
            cols = slice(c * MXU_TILE, (c + 1) * MXU_TILE)
            hg = jnp.dot(u, wg_ref[:, cols], preferred_element_type=F32)
            hu = jnp.dot(u, wu_ref[:, cols], preferred_element_type=F32)
            a_scr[rows, cols] = (_silu(hg) * hu).astype(BF16)
        y = jnp.dot(a_scr[rows, :], wo_ref[...], preferred_element_type=F32)
        o_ref[0, rows, :] = h + (MACARON_W * gate) * _rms(y, g_ref[2 * s + 1:2 * s + 2, :])


def _ffn(h, mod, g, w_in, w_out, *, layer, k, s, tm, mod_row):
    bsz, l, _ = h.shape
    resident = dict(pipeline_mode=pl.Buffered(1))
    return pl.pallas_call(
        functools.partial(_ffn_kernel, s=s, tm=tm),
        grid=(bsz, l // tm),
        in_specs=[
            pl.BlockSpec((1, tm, D_MODEL), lambda b, t: (b, t, 0)),
            pl.BlockSpec((1, N_MOD, D_MODEL), lambda b, t: (mod_row(b), 0, 0)),
            _const_spec((6, D_MODEL)),
            pl.BlockSpec((None, None, D_MODEL, FFN_DIM), lambda b, t: (layer, k, 0, 0), **resident),
            pl.BlockSpec((None, None, D_MODEL, FFN_DIM), lambda b, t: (layer, k, 0, 1), **resident),
            pl.BlockSpec((None, None, FFN_DIM, D_MODEL), lambda b, t: (layer, k, 0, 0), **resident),
        ],
        out_specs=pl.BlockSpec((1, tm, D_MODEL), lambda b, t: (b, t, 0)),
        out_shape=jax.ShapeDtypeStruct(h.shape, F32),
        scratch_shapes=[pltpu.VMEM((tm, FFN_DIM), BF16)],
        compiler_params=_cparams("parallel", "parallel"),
        name="ffn",
    )(h, mod, g, w_in, w_in, w_out)


def _ssd_in_kernel(xm_ref, xp_ref, xn_ref, mod_ref, g_ref, wz_ref, wxa_ref, wxb_ref, wdt_ref, cw_ref, cb_ref,
                   dtb_ref, z_ref, xc_ref, bt_ref, dt_ref, u_scr, p_scr, o_scr, *, tm):
    t = pl.program_id(1)
    nt = pl.num_programs(1)
    shift = mod_ref[0, 3:4, :]
    scale = mod_ref[0, 4:5, :]
    g_pre = g_ref[2:3, :]

    def prep(h):
        return _rms(h, g_pre) * (1.0 + scale) + shift

    u_scr[0:HALO, :] = jnp.where(t == 0, 0.0, prep(xp_ref[0])).astype(BF16)
    u_scr[HALO:HALO + tm, :] = prep(xm_ref[0]).astype(BF16)
    u_scr[HALO + tm:, :] = jnp.where(t == nt - 1, 0.0, prep(xn_ref[0])).astype(BF16)
    um = u_scr[HALO:HALO + tm, :]
    dt_raw = jnp.dot(um, wdt_ref[...], preferred_element_type=F32) + dtb_ref[...]
    dt_ref[0] = jnp.maximum(dt_raw, 0.0) + jnp.log1p(jnp.exp(-jnp.abs(dt_raw)))

    nblk = SSD_CONV_DIM // LANES
    npair = nblk // 2
    pairs_a = SSD_INNER // MXU_TILE
    groups8 = tm // (SUBLANES * SUBLANES)

    def project(pair):
        if pair < pairs_a:
            w = wxa_ref[:, pair * MXU_TILE:(pair + 1) * MXU_TILE]
        else:
            w = wxb_ref[:, (pair - pairs_a) * MXU_TILE:(pair - pairs_a + 1) * MXU_TILE]
        p = jnp.dot(u_scr[...], w, preferred_element_type=F32)
        p_scr[2 * pair] = p[:, :LANES]
        p_scr[2 * pair + 1] = p[:, LANES:]

    def conv_block(blk):
        cols = slice(blk * LANES, (blk + 1) * LANES)
        wk = [jnp.broadcast_to(cw_ref[k:k + 1, cols], (SUBLANES, LANES)) for k in range(SSD_CONV)]
        bias = jnp.broadcast_to(cb_ref[:, cols], (SUBLANES, LANES))
        slot = blk % 2
        for m in range(groups8):
            base = HALO - SSD_CONV // 2 + SUBLANES * SUBLANES * m
            slabs = [p_scr[blk, pl.ds(base + r, SUBLANES, stride=SUBLANES), :]
                     for r in range(SSD_CONV - 1)]
            for s in range(SUBLANES):
                slabs.append(p_scr[blk, pl.ds(base + s + SSD_CONV - 1, SUBLANES, stride=SUBLANES), :])
                acc = bias
                for k in range(SSD_CONV):
                    acc = acc + wk[k] * slabs[s + k]
                o_scr[slot, pl.ds(SUBLANES * SUBLANES * m + s, SUBLANES, stride=SUBLANES), :] = _silu(acc)
        val = o_scr[slot]
        nx = SSD_INNER // LANES
        nb = SSD_GN // LANES
        if blk < nx:
            xc_ref[0, :, cols] = val.astype(BF16)
        elif blk < nx + nb:
            rows = slice((blk - nx) * LANES, (blk - nx + 1) * LANES)
            val_t = val.T.astype(BF16)
            for ck in range(tm // SSD_CHUNK):
                bt_ref[0, ck, rows, :] = val_t[:, ck * SSD_CHUNK:(ck + 1) * SSD_CHUNK]
        else:
            oc = slice((blk - nb) * LANES, (blk - nb + 1) * LANES)
            xc_ref[0, :, oc] = val.astype(BF16)

    project(0)
    z_ref[0] = jnp.dot(um, wz_ref[...], preferred_element_type=F32).astype(BF16)
    for pair in range(npair):
        conv_block(2 * pair)
        if pair + 1 < npair:
            project(pair + 1)
        conv_block(2 * pair + 1)


def _ssd_in(h, mod, g, w_in, conv_w, conv_b, dt_bias, *, tm, mod_row):
    bsz, l, _ = h.shape
    nh = tm // HALO
    last = l // HALO - 1
    dt_pad = LANES - 2 * SSD_HEADS
    wdt = jnp.pad(w_in[:, SSD_INNER + SSD_CONV_DIM:], ((0, 0), (0, dt_pad)))
    dt_bias = jnp.pad(dt_bias, (0, dt_pad))
    wblk = lambda j: pl.BlockSpec((D_MODEL, SSD_INNER), lambda b, t: (0, j))
    return pl.pallas_call(
        functools.partial(_ssd_in_kernel, tm=tm),
        grid=(bsz, l // tm),
        in_specs=[
            pl.BlockSpec((1, tm, D_MODEL), lambda b, t: (b, t, 0)),
            pl.BlockSpec((1, HALO, D_MODEL), lambda b, t: (b, jnp.maximum(t * nh - 1, 0), 0)),
            pl.BlockSpec((1, HALO, D_MODEL), lambda b, t: (b, jnp.minimum((t + 1) * nh, last), 0)),
            pl.BlockSpec((1, N_MOD, D_MODEL), lambda b, t: (mod_row(b), 0, 0)),
            _const_spec((6, D_MODEL)),
            wblk(0), wblk(1), wblk(2),
            _const_spec((D_MODEL, LANES)),
            _const_spec((SSD_CONV, SSD_CONV_DIM)),
            _const_spec((1, SSD_CONV_DIM)),
            _const_spec((1, LANES)),
        ],
        out_specs=[
            pl.BlockSpec((1, tm, SSD_INNER), lambda b, t: (b, t, 0)),
            pl.BlockSpec((1, tm, SSD_INNER + SSD_GN), lambda b, t: (b, t, 0)),
            pl.BlockSpec((1, tm // SSD_CHUNK, SSD_GN, SSD_CHUNK), lambda b, t: (b, t, 0, 0)),
            pl.BlockSpec((1, tm, LANES), lambda b, t: (b, t, 0)),
        ],
        out_shape=[
            jax.ShapeDtypeStruct((bsz, l, SSD_INNER), BF16),
            jax.ShapeDtypeStruct((bsz, l, SSD_INNER + SSD_GN), BF16),
            jax.ShapeDtypeStruct((bsz, l // SSD_CHUNK, SSD_GN, SSD_CHUNK), BF16),
            jax.ShapeDtypeStruct((bsz, l, LANES), F32),
        ],
        scratch_shapes=[
            pltpu.VMEM((tm + 2 * HALO, D_MODEL), BF16),
            pltpu.VMEM((SSD_CONV_DIM // LANES, tm + 2 * HALO, LANES), F32),
            pltpu.VMEM((2, tm, LANES), F32),
        ],
        compiler_params=_cparams("parallel", "parallel"),
        name="ssd_in",
    )(h, h, h, mod, g, w_in, w_in, w_in, wdt, conv_w, conv_b.reshape(1, -1), dt_bias.reshape(1, -1))


def _split_bf16(v, n):
    parts = []
    for _ in range(n):
        p = v.astype(BF16).astype(F32)
        parts.append(p)
        v = v - p
    return parts


def _scan_tables():
    q = SSD_CHUNK
    r = jnp.arange(LANES)[:, None]
    ca = jnp.arange(SSD_HEADS * q)[None, :]
    ind_a = (((r % SSD_HEADS) == (ca // q)) & (r < 3 * SSD_HEADS)).astype(F32)
    j = ca % q
    mask_f = jnp.where(j > r, NEG_BIG, 0.0)
    mask_b = jnp.where(j < r, NEG_BIG, 0.0)
    seg_tab = jnp.stack([jnp.concatenate([ind_a, mask_f], axis=0),
                         jnp.concatenate([ind_a, mask_b], axis=0)]).astype(BF16)
    ce = jnp.arange(2 * SSD_INNER)[None, :]
    ind_ew = ((r % SSD_HEADS) == ((ce % SSD_INNER) // SSD_HEAD_DIM)) \
        & ((r // (2 * SSD_HEADS)) == (ce // SSD_INNER))
    return seg_tab, ind_ew.astype(BF16)


def _scan_dir(d, ro, xc_ref, bt_ref, dt_ref, alog_ref, dskip_ref, tab_scr, indew_ref, y_ref, h_ref):
    q = SSD_CHUNK
    nh = SSD_HEADS
    rws = slice(ro, ro + q)
    dt = dt_ref[0, rws, nh * d:nh * (d + 1)]
    a_neg = -jnp.exp(alog_ref[d:d + 1, :])
    row = lax.broadcasted_iota(jnp.int32, (q, q), 0)
    col = lax.broadcasted_iota(jnp.int32, (q, q), 1)
    tri = (row >= col) if d == 0 else (row <= col)
    pad_a = jnp.zeros((q, LANES - 3 * nh), F32)
    cs = jnp.dot(tri.astype(BF16),
                 jnp.concatenate(_split_bf16(dt * a_neg, 3) + [pad_a], axis=1).astype(BF16),
                 preferred_element_type=F32)
    a = cs[:, 0:nh] + cs[:, nh:2 * nh] + cs[:, 2 * nh:3 * nh]
    a_end = a[q - 1:q, :] if d == 0 else a[0:1, :]
    e_in = jnp.exp(a)
    w_end = dt * jnp.exp(a_end - a)
    r_nat = LOG2E * (jnp.maximum(jnp.log(dt), NEG_BIG) - a)
    r_t = jnp.concatenate([r_nat, jnp.zeros((q, LANES - nh), F32)], axis=1).T[0:nh]
    r_parts = _split_bf16(r_t, 3)
    prow = lax.broadcasted_iota(jnp.int32, (BF16_ROWS, q), 0)
    for hd in range(nh):
        tile = jnp.where(prow == 0, r_parts[0][hd:hd + 1, :],
                         jnp.where(prow == 1, r_parts[1][hd:hd + 1, :],
                                   jnp.where(prow == 2, r_parts[2][hd:hd + 1, :], 0.0)))
        tab_scr[d, 3 * nh:3 * nh + BF16_ROWS, q * hd:q * (hd + 1)] = tile.astype(BF16)
    ones3 = (lax.broadcasted_iota(jnp.int32, (q, LANES - 3 * nh), 1) < 3).astype(F32)
    eye = (row == col).astype(F32)
    seg_lhs = jnp.concatenate(_split_bf16(LOG2E * a, 3) + [ones3, eye], axis=1).astype(BF16)
    ew_parts = jnp.concatenate(_split_bf16(e_in, 2) + _split_bf16(w_end, 2), axis=1).astype(BF16)
    low_half = lax.broadcasted_iota(jnp.int32, (q, 2 * SSD_HEAD_DIM), 1) < SSD_HEAD_DIM

    gw = SSD_HPG * SSD_HEAD_DIM

    def group(g):
        gs = slice(gw * g, gw * (g + 1))
        c_g = xc_ref[0, rws, SSD_INNER + SSD_STATE * g:SSD_INNER + SSD_STATE * (g + 1)]
        bt_g = bt_ref[0, ro // q, SSD_STATE * g:SSD_STATE * (g + 1), :]
        x_g = xc_ref[0, rws, gs]
        ht_g = h_ref[0, d, :, gs]
        cb = jnp.dot(c_g, bt_g, preferred_element_type=F32)
        seg = jnp.dot(seg_lhs, tab_scr[d, :, SSD_HPG * q * g:SSD_HPG * q * (g + 1)],
                      preferred_element_type=F32)
        e_rep = jnp.dot(ew_parts, indew_ref[:, gs], preferred_element_type=F32)
        w_rep = jnp.dot(ew_parts, indew_ref[:, SSD_INNER + gw * g:SSD_INNER + gw * (g + 1)],
                        preferred_element_type=F32)
        y_g = jnp.dot(c_g, ht_g.astype(BF16), preferred_element_type=F32) * e_rep
        if d == 0:
            y_g = y_g + dskip_ref[:, gs] * x_g.astype(F32)
        ms = [(cb * jnp.exp2(seg[:, q * k:q * (k + 1)])).astype(BF16) for k in range(SSD_HPG)]
        y_pairs = []
        for pr in range(SSD_HPG // 2):
            x_pair = x_g[:, 2 * SSD_HEAD_DIM * pr:2 * SSD_HEAD_DIM * (pr + 1)]
            zero = jnp.zeros_like(x_pair)
            x_bd = jnp.concatenate([jnp.where(low_half, x_pair, zero),
                                    jnp.where(low_half, zero, x_pair)], axis=0)
            m_pair = jnp.concatenate([ms[2 * pr], ms[2 * pr + 1]], axis=1)
            y_pairs.append(jnp.dot(m_pair, x_bd, preferred_element_type=F32))
        y_ref[0, rws, gs] = (y_g + jnp.concatenate(y_pairs, axis=1)).astype(BF16)
        xw = (x_g.astype(F32) * w_rep).astype(BF16)
        s_g = jnp.dot(bt_g, xw, preferred_element_type=F32)
        dec_row = e_rep[q - 1:q, :] if d == 0 else e_rep[0:1, :]
        h_ref[0, d, :, gs] = ht_g * dec_row + s_g

    return group


def _ssd_scan_kernel(xf_ref, xb_ref, btf_ref, btb_ref, dtf_ref, dtb_ref, alog_ref, dskip_ref,
                     tab_ref, indew_ref, *rest, has_h0, chunks):
    if has_h0:
        h0_ref, yf_ref, yb_ref, h_ref, tab_scr = rest
    else:
        yf_ref, yb_ref, h_ref, tab_scr = rest

    @pl.when(pl.program_id(1) == 0)
    def _():
        h_ref[...] = h0_ref[...] if has_h0 else jnp.zeros(h_ref.shape, F32)
        tab_scr[...] = tab_ref[...]

    for sub in range(chunks):
        ro_f = sub * SSD_CHUNK
        ro_b = (chunks - 1 - sub) * SSD_CHUNK
        fwd_group = _scan_dir(0, ro_f, xf_ref, btf_ref, dtf_ref, alog_ref, dskip_ref, tab_scr, indew_ref,
                              yf_ref, h_ref)
        bwd_group = _scan_dir(1, ro_b, xb_ref, btb_ref, dtb_ref, alog_ref, dskip_ref, tab_scr, indew_ref,
                              yb_ref, h_ref)
        for g in range(SSD_GROUPS):
            fwd_group(g)
            bwd_group(g)


def _ssd_scan(xc, bt, dt, a_log, d_skip, h0=None):
    bsz, l, _ = xc.shape
    has_h0 = h0 is not None
    chunks = min(SCAN_CHUNKS, l // SSD_CHUNK)
    rows = chunks * SSD_CHUNK
    nc = l // rows
    fwd = lambda b, c: (b, c, 0)
    bwd = lambda b, c: (b, nc - 1 - c, 0)
    fwd_t = lambda b, c: (b, c, 0, 0)
    bwd_t = lambda b, c: (b, nc - 1 - c, 0, 0)
    st_spec = pl.BlockSpec((1, 2, SSD_STATE, SSD_INNER), lambda b, c: (b, 0, 0, 0))
    seg_tab, ind_ew = _scan_tables()
    return pl.pallas_call(
        functools.partial(_ssd_scan_kernel, has_h0=has_h0, chunks=chunks),
        grid=(bsz, nc),
        in_specs=[
            pl.BlockSpec((1, rows, SSD_INNER + SSD_GN), fwd),
            pl.BlockSpec((1, rows, SSD_INNER + SSD_GN), bwd),
            pl.BlockSpec((1, chunks, SSD_GN, SSD_CHUNK), fwd_t),
            pl.BlockSpec((1, chunks, SSD_GN, SSD_CHUNK), bwd_t),
            pl.BlockSpec((1, rows, LANES), fwd),
            pl.BlockSpec((1, rows, LANES), bwd),
            _const_spec((2, SSD_HEADS)),
            _const_spec((1, SSD_INNER)),
            _const_spec(seg_tab.shape),
            _const_spec(ind_ew.shape),
        ] + ([st_spec] if has_h0 else []),
        out_specs=[
            pl.BlockSpec((1, rows, SSD_INNER), fwd),
            pl.BlockSpec((1, rows, SSD_INNER), bwd),
            st_spec,
        ],
        out_shape=[
            jax.ShapeDtypeStruct((bsz, l, SSD_INNER), BF16),
            jax.ShapeDtypeStruct((bsz, l, SSD_INNER), BF16),
            jax.ShapeDtypeStruct((bsz, 2, SSD_STATE, SSD_INNER), F32),
        ],
        scratch_shapes=[pltpu.VMEM(seg_tab.shape, BF16)],
        compiler_params=_cparams("parallel", "arbitrary"),
        name="ssd_scan",
    )(xc, xc, bt, bt, dt, dt, a_log, d_skip, seg_tab, ind_ew, *([h0] if has_h0 else []))


def _ssd_out_kernel(h_ref, yf_ref, yb_ref, z_ref, mod_ref, g_ref, ng_ref, wo_ref, o_ref, v_scr):
    gw = SSD_INNER // SSD_GROUPS
    for g in range(SSD_GROUPS):
        cols = slice(g * gw, (g + 1) * gw)
        y = (yf_ref[0, :, cols].astype(F32) + yb_ref[0, :, cols].astype(F32)) \
            * _silu(z_ref[0, :, cols].astype(F32))
        y = y * lax.rsqrt(jnp.mean(y * y, axis=-1, keepdims=True) + EPS)
        v_scr[:, cols] = (y * ng_ref[:, cols]).astype(BF16)
    out = jnp.dot(v_scr[...], wo_ref[...], preferred_element_type=F32)
    o_ref[0] = h_ref[0] + mod_ref[0, 5:6, :] * _rms(out, g_ref[3:4, :])


def _ssd_out(h, yf, yb, z, mod, g, norm_g, w_out, *, tm, mod_row):
    bsz, l, _ = h.shape
    tok = lambda w: pl.BlockSpec((1, tm, w), lambda b, t: (b, t, 0))
    return pl.pallas_call(
        _ssd_out_kernel,
        grid=(bsz, l // tm),
        in_specs=[
            tok(D_MODEL), tok(SSD_INNER), tok(SSD_INNER), tok(SSD_INNER),
            pl.BlockSpec((1, N_MOD, D_MODEL), lambda b, t: (mod_row(b), 0, 0)),
            _const_spec((6, D_MODEL)),
            _const_spec((1, SSD_INNER)),
            _const_spec((SSD_INNER, D_MODEL)),
        ],
        out_specs=tok(D_MODEL),
        out_shape=jax.ShapeDtypeStruct(h.shape, F32),
        scratch_shapes=[pltpu.VMEM((tm, SSD_INNER), BF16)],
        compiler_params=_cparams("parallel", "parallel"),
        name="ssd_out",
    )(h, yf, yb, z, mod, g, norm_g.reshape(1, -1), w_out)


def _gmlp_kernel(h_ref, mod_ref, g_ref, wu_ref, wv_ref, vg_ref, vb_ref, ws_ref, bs_ref, wo_ref, o_ref,
                 p_scr, *, tm):
    rb = min(tm, ROW_BLOCK)
    for r in range(tm // rb):
        blk = slice(r * rb, (r + 1) * rb)
        h = h_ref[0, blk, :]
        u = (_rms(h, g_ref[2:3, :]) * (1.0 + mod_ref[0, 4:5, :]) + mod_ref[0, 3:4, :]).astype(BF16)
        gv = _gelu_tanh(jnp.dot(u, wv_ref[...], preferred_element_type=F32))
        mu = jnp.mean(gv, axis=-1, keepdims=True)
        var = jnp.mean(jnp.square(gv - mu), axis=-1, keepdims=True)
        gvn = ((gv - mu) * lax.rsqrt(var + EPS) * vg_ref[...] + vb_ref[...]).astype(BF16)
        for g in range(GM_GROUPS):
            cols = slice(g * GM_GROUP_DIM, (g + 1) * GM_GROUP_DIM)
            gu = _gelu_tanh(jnp.dot(u, wu_ref[:, cols], preferred_element_type=F32))
            for c in range(rb // GM_CHUNK):
                rows = slice(c * GM_CHUNK, (c + 1) * GM_CHUNK)
                s = jnp.dot(ws_ref[g], gvn[rows, cols], preferred_element_type=F32) + bs_ref[:, cols]
                p_scr[r * rb + c * GM_CHUNK:r * rb + (c + 1) * GM_CHUNK, cols] = (gu[rows, :] * s).astype(BF16)
        y = jnp.dot(p_scr[blk, :], wo_ref[...], preferred_element_type=F32)
        o_ref[0, blk, :] = h + mod_ref[0, 5:6, :] * _rms(y, g_ref[3:4, :])


def _gmlp(h, mod, g, w_in, v_g, v_b, w_s, b_s_full, w_out, *, tm, mod_row):
    bsz, l, _ = h.shape
    resident = dict(pipeline_mode=pl.Buffered(1))
    return pl.pallas_call(
        functools.partial(_gmlp_kernel, tm=tm),
        grid=(bsz, l // tm),
        in_specs=[
            pl.BlockSpec((1, tm, D_MODEL), lambda b, t: (b, t, 0)),
            pl.BlockSpec((1, N_MOD, D_MODEL), lambda b, t: (mod_row(b), 0, 0)),
            _const_spec((6, D_MODEL)),
            pl.BlockSpec((D_MODEL, GM_INNER), lambda b, t: (0, 0), **resident),
            pl.BlockSpec((D_MODEL, GM_INNER), lambda b, t: (0, 1), **resident),
            _const_spec((1, GM_INNER)),
            _const_spec((1, GM_INNER)),
            _const_spec((GM_GROUPS, GM_CHUNK, GM_CHUNK)),
            _const_spec((GM_CHUNK, GM_INNER)),
            pl.BlockSpec((GM_INNER, D_MODEL), lambda b, t: (0, 0), **resident),
        ],
        out_specs=pl.BlockSpec((1, tm, D_MODEL), lambda b, t: (b, t, 0)),
        out_shape=jax.ShapeDtypeStruct(h.shape, F32),
        scratch_shapes=[pltpu.VMEM((tm, GM_INNER), BF16)],
        compiler_params=_cparams("parallel", "parallel"),
        name="gmlp",
    )(h, mod, g, w_in, w_in, v_g.reshape(1, -1), v_b.reshape(1, -1), w_s, b_s_full, w_out)


def kernel(x, c, ctx, c_ctx, ada_w, ada_b, norm_g, ffn_w_in, ffn_w_out, ssd_w_in, ssd_conv_w, ssd_conv_b,
           ssd_dt_bias, ssd_A_log, ssd_D, ssd_norm_g, ssd_w_out, gm_w_in, gm_v_g, gm_v_b, gm_w_s, gm_b_s,
           gm_w_out):
    bsz, seq, _ = x.shape
    ctx_len = ctx.shape[1]
    cvec = jnp.zeros((MOD_ROWS, D_MODEL), F32).at[:bsz].set(c).at[bsz].set(c_ctx)
    mod_all = _ada_mod(cvec, ada_w, ada_b).reshape(DEPTH, MOD_ROWS, N_MOD, D_MODEL)

    x_row = lambda b: b
    ctx_row = lambda b: bsz
    tm_x = 512
    tm_ffn = 1024
    tm_c = ctx_len
    ffn_in = ffn_w_in.astype(BF16)
    ffn_out = ffn_w_out.astype(BF16)

    for i in range(DEPTH):
        use_ssd = (i % 2) == 0
        j = i // 2
        last = i == DEPTH - 1
        ctx_needed = (not last) or use_ssd
        ctx_full = not last
        mod = mod_all[i]
        g = norm_g[i]

        x = _ffn(x, mod, g, ffn_in, ffn_out, layer=i, k=0, s=0, tm=tm_ffn, mod_row=x_row)
        if ctx_needed:
            ctx = _ffn(ctx, mod, g, ffn_in, ffn_out, layer=i, k=0, s=0, tm=tm_c, mod_row=ctx_row)

        if use_ssd:
            w_in = ssd_w_in[j].astype(BF16)
            w_out = ssd_w_out[j].astype(BF16)
            d_skip = jnp.repeat(ssd_D[j], SSD_HEAD_DIM).reshape(1, SSD_INNER)
            dtb = ssd_dt_bias[j].reshape(-1)

            def branch(h, h0, tm, mod_row):
                z, xc, bt, dt = _ssd_in(h, mod, g, w_in, ssd_conv_w[j], ssd_conv_b[j], dtb,
                                        tm=tm, mod_row=mod_row)
                yf, yb, hT = _ssd_scan(xc, bt, dt, ssd_A_log[j], d_skip, h0)
                return yf, yb, z, hT

            cyf, cyb, cz, h_ctx = branch(ctx, None, tm_c, ctx_row)
            xyf, xyb, xz, _ = branch(x, h_ctx, tm_x, x_row)
            x = _ssd_out(x, xyf, xyb, xz, mod, g, ssd_norm_g[j], w_out, tm=tm_x, mod_row=x_row)
            if ctx_full:
                ctx = _ssd_out(ctx, cyf, cyb, cz, mod, g, ssd_norm_g[j], w_out, tm=tm_c, mod_row=ctx_row)
        else:
            gw_in = gm_w_in[j].astype(BF16)
            gw_out = gm_w_out[j].astype(BF16)
            ws = gm_w_s[j].astype(BF16)
            bs_full = jnp.repeat(gm_b_s[j].T, GM_GROUP_DIM, axis=1)
            x = _gmlp(x, mod, g, gw_in, gm_v_g[j], gm_v_b[j], ws, bs_full, gw_out, tm=tm_ffn, mod_row=x_row)
            if ctx_full:
                ctx = _gmlp(ctx, mod, g, gw_in, gm_v_g[j], gm_v_b[j], ws, bs_full, gw_out,
                            tm=tm_c, mod_row=ctx_row)

        x = _ffn(x, mod, g, ffn_in, ffn_out, layer=i, k=1, s=2, tm=tm_ffn, mod_row=x_row)
        if ctx_full:
            ctx = _ffn(ctx, mod, g, ffn_in, ffn_out, layer=i, k=1, s=2, tm=tm_c, mod_row=ctx_row)
    return x
```

```python
import functools
import math

import jax
import jax.numpy as jnp
from jax import lax
from jax.experimental import pallas as pl
from jax.experimental.pallas import tpu as pltpu

F32 = jnp.float32
BF16 = jnp.bfloat16

D_MODEL = 1024
DEPTH = 2
N_MOD = 9
MACARON_W = 0.5
EPS = 1e-6
FFN_DIM = 2816

SSD_INNER = 2048
SSD_HEAD_DIM = 64
SSD_HEADS = 32
SSD_GROUPS = 8
SSD_HPG = 4
SSD_STATE = 128
SSD_CONV = 5
SSD_CHUNK = 128
SSD_GN = SSD_GROUPS * SSD_STATE
SSD_CONV_DIM = SSD_INNER + 2 * SSD_GN

GM_CHUNK = 128
GM_INNER = 2048
GM_GROUPS = 8
GM_GROUP_DIM = 256

MXU_TILE = 256
LANES = 128
SUBLANES = 8
BF16_ROWS = 16
SCAN_CHUNKS = 4
ROW_BLOCK = 512
HALO = 16
VMEM_LIMIT = 56 * 1024 * 1024
MOD_ROWS = 16
LOG2E = math.log2(math.e)
NEG_BIG = -1e30


def _cparams(*sem):
    return pltpu.CompilerParams(dimension_semantics=sem, vmem_limit_bytes=VMEM_LIMIT)


def _rms(x, g):
    return x * lax.rsqrt(jnp.mean(x * x, axis=-1, keepdims=True) + EPS) * g


def _silu(x):
    return x * jax.nn.sigmoid(x)


def _gelu_tanh(x):
    k = -2.0 * LOG2E * math.sqrt(2.0 / math.pi)
    t = x * (k + (k * 0.044715) * (x * x))
    return x / (1.0 + jnp.exp2(t))


def _const_spec(shape):
    nd = len(shape)
    return pl.BlockSpec(shape, lambda *_: (0,) * nd)


def _ada_kernel(c_ref, w_ref, b_ref, o_ref):
    cs = _silu(c_ref[...]).astype(BF16)
    o_ref[...] = jnp.dot(cs, w_ref[...].astype(BF16), preferred_element_type=F32) + b_ref[...]


def _ada_mod(cvec, ada_w, ada_b):
    nd = N_MOD * D_MODEL
    bn = 1024
    return pl.pallas_call(
        _ada_kernel,
        grid=(DEPTH, nd // bn),
        in_specs=[
            pl.BlockSpec((MOD_ROWS, D_MODEL), lambda i, j: (0, 0)),
            pl.BlockSpec((None, D_MODEL, bn), lambda i, j: (i, 0, j)),
            pl.BlockSpec((None, 1, bn), lambda i, j: (i, 0, j)),
        ],
        out_specs=pl.BlockSpec((None, MOD_ROWS, bn), lambda i, j: (i, 0, j)),
        out_shape=jax.ShapeDtypeStruct((DEPTH, MOD_ROWS, nd), F32),
        compiler_params=_cparams("parallel", "parallel"),
        name="ada_mod",
    )(cvec, ada_w, ada_b.reshape(DEPTH, 1, nd))


def _ffn_kernel(h_ref, mod_ref, g_ref, wg_ref, wu_ref, wo_ref, o_ref, a_scr, *, s, tm):
    shift = mod_ref[0, 3 * s:3 * s + 1, :]
    scale = mod_ref[0, 3 * s + 1:3 * s + 2, :]
    gate = mod_ref[0, 3 * s + 2:3 * s + 3, :]
    rb = min(tm, ROW_BLOCK)
    for r in range(tm // rb):
        rows = slice(r * rb, (r + 1) * rb)
        h = h_ref[0, rows, :]
        u = (_rms(h, g_ref[2 * s:2 * s + 1, :]) * (1.0 + scale) + shift).astype(BF16)
        for c in range(FFN_DIM // MXU_TILE):
            cols = slice(c * MXU_TILE, (c + 1) * MXU_TILE)
            hg = jnp.dot(u, wg_ref[:, cols], preferred_element_type=F32)
            hu = jnp.dot(u, wu_ref[:, cols], preferred_element_type=F32)
            a_scr[rows, cols] = (_silu(hg) * hu).astype(BF16)
        y = jnp.dot(a_scr[rows, :], wo_ref[...], preferred_element_type=F32)
        o_ref[0, rows, :] = h + (MACARON_W * gate) * _rms(y, g_ref[2 * s + 1:2 * s + 2, :])


def _ffn(h, mod, g, w_in, w_out, *, layer, k, s, tm, mod_row):
    bsz, l, _ = h.shape
    resident = dict(pipeline_mode=pl.Buffered(1))
    return pl.pallas_call(
        functools.partial(_ffn_kernel, s=s, tm=tm),
        grid=(bsz, l // tm),
        in_specs=[
            pl.BlockSpec((1, tm, D_MODEL), lambda b, t: (b, t, 0)),
            pl.BlockSpec((1, N_MOD, D_MODEL), lambda b, t: (mod_row(b), 0, 0)),
            _const_spec((6, D_MODEL)),
            pl.BlockSpec((None, None, D_MODEL, FFN_DIM), lambda b, t: (layer, k, 0, 0), **resident),
            pl.BlockSpec((None, None, D_MODEL, FFN_DIM), lambda b, t: (layer, k, 0, 1), **resident),
            pl.BlockSpec((None, None, FFN_DIM, D_MODEL), lambda b, t: (layer, k, 0, 0), **resident),
        ],
        out_specs=pl.BlockSpec((1, tm, D_MODEL), lambda b, t: (b, t, 0)),
        out_shape=jax.ShapeDtypeStruct(h.shape, F32),
        scratch_shapes=[pltpu.VMEM((tm, FFN_DIM), BF16)],
        compiler_params=_cparams("parallel", "parallel"),
        name="ffn",
    )(h, mod, g, w_in, w_in, w_out)


def _ssd_in_kernel(xm_ref, xp_ref, xn_ref, mod_ref, g_ref, wz_ref, wxa_ref, wxb_ref, wdt_ref, cw_ref, cb_ref,
                   dtb_ref, z_ref, xc_ref, bt_ref, dt_ref, u_scr, p_scr, o_scr, *, tm):
    t = pl.program_id(1)
    nt = pl.num_programs(1)
    shift = mod_ref[0, 3:4, :]
    scale = mod_ref[0, 4:5, :]
    g_pre = g_ref[2:3, :]

    def prep(h):
        return _rms(h, g_pre) * (1.0 + scale) + shift

    u_scr[0:HALO, :] = jnp.where(t == 0, 0.0, prep(xp_ref[0])).astype(BF16)
    u_scr[HALO:HALO + tm, :] = prep(xm_ref[0]).astype(BF16)
    u_scr[HALO + tm:, :] = jnp.where(t == nt - 1, 0.0, prep(xn_ref[0])).astype(BF16)
    um = u_scr[HALO:HALO + tm, :]
    dt_raw = jnp.dot(um, wdt_ref[...], preferred_element_type=F32) + dtb_ref[...]
    dt_ref[0] = jnp.maximum(dt_raw, 0.0) + jnp.log1p(jnp.exp(-jnp.abs(dt_raw)))

    nblk = SSD_CONV_DIM // LANES
    npair = nblk // 2
    pairs_a = SSD_INNER // MXU_TILE
    groups8 = tm // (SUBLANES * SUBLANES)

    def project(pair):
        if pair < pairs_a:
            w = wxa_ref[:, pair * MXU_TILE:(pair + 1) * MXU_TILE]
        else:
            w = wxb_ref[:, (pair - pairs_a) * MXU_TILE:(pair - pairs_a + 1) * MXU_TILE]
        p = jnp.dot(u_scr[...], w, preferred_element_type=F32)
        p_scr[2 * pair] = p[:, :LANES]
        p_scr[2 * pair + 1] = p[:, LANES:]

    def conv_block(blk):
        cols = slice(blk * LANES, (blk + 1) * LANES)
        wk = [jnp.broadcast_to(cw_ref[k:k + 1, cols], (SUBLANES, LANES)) for k in range(SSD_CONV)]
        bias = jnp.broadcast_to(cb_ref[:, cols], (SUBLANES, LANES))
        slot = blk % 2
        for m in range(groups8):
            base = HALO - SSD_CONV // 2 + SUBLANES * SUBLANES * m
            slabs = [p_scr[blk, pl.ds(base + r, SUBLANES, stride=SUBLANES), :]
                     for r in range(SSD_CONV - 1)]
            for s in range(SUBLANES):
                slabs.append(p_scr[blk, pl.ds(base + s + SSD_CONV - 1, SUBLANES, stride=SUBLANES), :])
                acc = bias
                for k in range(SSD_CONV):
                    acc = acc + wk[k] * slabs[s + k]
                o_scr[slot, pl.ds(SUBLANES * SUBLANES * m + s, SUBLANES, stride=SUBLANES), :] = _silu(acc)
        val = o_scr[slot]
        nx = SSD_INNER // LANES
        nb = SSD_GN // LANES
        if blk < nx:
            xc_ref[0, :, cols] = val.astype(BF16)
        elif blk < nx + nb:
            rows = slice((blk - nx) * LANES, (blk - nx + 1) * LANES)
            val_t = val.T.astype(BF16)
            for ck in range(tm // SSD_CHUNK):
                bt_ref[0, ck, rows, :] = val_t[:, ck * SSD_CHUNK:(ck + 1) * SSD_CHUNK]
        else:
            oc = slice((blk - nb) * LANES, (blk - nb + 1) * LANES)
            xc_ref[0, :, oc] = val.astype(BF16)

    project(0)
    z_ref[0] = jnp.dot(um, wz_ref[...], preferred_element_type=F32).astype(BF16)
    for pair in range(npair):
        conv_block(2 * pair)
        if pair + 1 < npair:
            project(pair + 1)
        conv_block(2 * pair + 1)


def _ssd_in(h, mod, g, w_in, conv_w, conv_b, dt_bias, *, tm, mod_row):
    bsz, l, _ = h.shape
    nh = tm // HALO
    last = l // HALO - 1
    dt_pad = LANES - 2 * SSD_HEADS
    wdt = jnp.pad(w_in[:, SSD_INNER + SSD_CONV_DIM:], ((0, 0), (0, dt_pad)))
    dt_bias = jnp.pad(dt_bias, (0, dt_pad))
    wblk = lambda j: pl.BlockSpec((D_MODEL, SSD_INNER), lambda b, t: (0, j))
    return pl.pallas_call(
        functools.partial(_ssd_in_kernel, tm=tm),
        grid=(bsz, l // tm),
        in_specs=[
            pl.BlockSpec((1, tm, D_MODEL), lambda b, t: (b, t, 0)),
            pl.BlockSpec((1, HALO, D_MODEL), lambda b, t: (b, jnp.maximum(t * nh - 1, 0), 0)),
            pl.BlockSpec((1, HALO, D_MODEL), lambda b, t: (b, jnp.minimum((t + 1) * nh, last), 0)),
            pl.BlockSpec((1, N_MOD, D_MODEL), lambda b, t: (mod_row(b), 0, 0)),
            _const_spec((6, D_MODEL)),
            wblk(0), wblk(1), wblk(2),
            _const_spec((D_MODEL, LANES)),
            _const_spec((SSD_CONV, SSD_CONV_DIM)),
            _const_spec((1, SSD_CONV_DIM)),
            _const_spec((1, LANES)),
        ],
        out_specs=[
            pl.BlockSpec((1, tm, SSD_INNER), lambda b, t: (b, t, 0)),
            pl.BlockSpec((1, tm, SSD_INNER + SSD_GN), lambda b, t: (b, t, 0)),
            pl.BlockSpec((1, tm // SSD_CHUNK, SSD_GN, SSD_CHUNK), lambda b, t: (b, t, 0, 0)),
            pl.BlockSpec((1, tm, LANES), lambda b, t: (b, t, 0)),
        ],
        out_shape=[
            jax.ShapeDtypeStruct((bsz, l, SSD_INNER), BF16),
            jax.ShapeDtypeStruct((bsz, l, SSD_INNER + SSD_GN), BF16),
            jax.ShapeDtypeStruct((bsz, l // SSD_CHUNK, SSD_GN, SSD_CHUNK), BF16),
            jax.ShapeDtypeStruct((bsz, l, LANES), F32),
        ],
        scratch_shapes=[
            pltpu.VMEM((tm + 2 * HALO, D_MODEL), BF16),
            pltpu.VMEM((SSD_CONV_DIM // LANES, tm + 2 * HALO, LANES), F32),
            pltpu.VMEM((2, tm, LANES), F32),
        ],
        compiler_params=_cparams("parallel", "parallel"),
        name="ssd_in",
    )(h, h, h, mod, g, w_in, w_in, w_in, wdt, conv_w, conv_b.reshape(1, -1), dt_bias.reshape(1, -1))


def _split_bf16(v, n):
    parts = []
    for _ in range(n):
        p = v.astype(BF16).astype(F32)
        parts.append(p)
        v = v - p
    return parts


def _scan_tables():
    q = SSD_CHUNK
    r = jnp.arange(LANES)[:, None]
    ca = jnp.arange(SSD_HEADS * q)[None, :]
    ind_a = (((r % SSD_HEADS) == (ca // q)) & (r < 3 * SSD_HEADS)).astype(F32)
    j = ca % q
    mask_f = jnp.where(j > r, NEG_BIG, 0.0)
    mask_b = jnp.where(j < r, NEG_BIG, 0.0)
    seg_tab = jnp.stack([jnp.concatenate([ind_a, mask_f], axis=0),
                         jnp.concatenate([ind_a, mask_b], axis=0)]).astype(BF16)
    ce = jnp.arange(2 * SSD_INNER)[None, :]
    ind_ew = ((r % SSD_HEADS) == ((ce % SSD_INNER) // SSD_HEAD_DIM)) \
        & ((r // (2 * SSD_HEADS)) == (ce // SSD_INNER))
    return seg_tab, ind_ew.astype(BF16)


def _scan_dir(d, ro, xc_ref, bt_ref, dt_ref, alog_ref, dskip_ref, tab_scr, indew_ref, y_ref, h_ref):
    q = SSD_CHUNK
    nh = SSD_HEADS
    rws = slice(ro, ro + q)
    dt = dt_ref[0, rws, nh * d:nh * (d + 1)]
    a_neg = -jnp.exp(alog_ref[d:d + 1, :])
    row = lax.broadcasted_iota(jnp.int32, (q, q), 0)
    col = lax.broadcasted_iota(jnp.int32, (q, q), 1)
    tri = (row >= col) if d == 0 else (row <= col)
    pad_a = jnp.zeros((q, LANES - 3 * nh), F32)
    cs = jnp.dot(tri.astype(BF16),
                 jnp.concatenate(_split_bf16(dt * a_neg, 3) + [pad_a], axis=1).astype(BF16),
                 preferred_element_type=F32)
    a = cs[:, 0:nh] + cs[:, nh:2 * nh] + cs[:, 2 * nh:3 * nh]
    a_end = a[q - 1:q, :] if d == 0 else a[0:1, :]
    e_in = jnp.exp(a)
    w_end = dt * jnp.exp(a_end - a)
    r_nat = LOG2E * (jnp.maximum(jnp.log(dt), NEG_BIG) - a)
    r_t = jnp.concatenate([r_nat, jnp.zeros((q, LANES - nh), F32)], axis=1).T[0:nh]
    r_parts = _split_bf16(r_t, 3)
    prow = lax.broadcasted_iota(jnp.int32, (BF16_ROWS, q), 0)
    for hd in range(nh):
        tile = jnp.where(prow == 0, r_parts[0][hd:hd + 1, :],
                         jnp.where(prow == 1, r_parts[1][hd:hd + 1, :],
                                   jnp.where(prow == 2, r_parts[2][hd:hd + 1, :], 0.0)))
        tab_scr[d, 3 * nh:3 * nh + BF16_ROWS, q * hd:q * (hd + 1)] = tile.astype(BF16)
    ones3 = (lax.broadcasted_iota(jnp.int32, (q, LANES - 3 * nh), 1) < 3).astype(F32)
    eye = (row == col).astype(F32)
    seg_lhs = jnp.concatenate(_split_bf16(LOG2E * a, 3) + [ones3, eye], axis=1).astype(BF16)
    ew_parts = jnp.concatenate(_split_bf16(e_in, 2) + _split_bf16(w_end, 2), axis=1).astype(BF16)
    low_half = lax.broadcasted_iota(jnp.int32, (q, 2 * SSD_HEAD_DIM), 1) < SSD_HEAD_DIM

    gw = SSD_HPG * SSD_HEAD_DIM

    def group(g):
        gs = slice(gw * g, gw * (g + 1))
        c_g = xc_ref[0, rws, SSD_INNER + SSD_STATE * g:SSD_INNER + SSD_STATE * (g + 1)]
        bt_g = bt_ref[0, ro // q, SSD_STATE * g:SSD_STATE * (g + 1), :]
        x_g = xc_ref[0, rws, gs]
        ht_g = h_ref[0, d, :, gs]
        cb = jnp.dot(c_g, bt_g, preferred_element_type=F32)
        seg = jnp.dot(seg_lhs, tab_scr[d, :, SSD_HPG * q * g:SSD_HPG * q * (g + 1)],
                      preferred_element_type=F32)
        e_rep = jnp.dot(ew_parts, indew_ref[:, gs], preferred_element_type=F32)
        w_rep = jnp.dot(ew_parts, indew_ref[:, SSD_INNER + gw * g:SSD_INNER + gw * (g + 1)],
                        preferred_element_type=F32)
        y_g = jnp.dot(c_g, ht_g.astype(BF16), preferred_element_type=F32) * e_rep
        if d == 0:
            y_g = y_g + dskip_ref[:, gs] * x_g.astype(F32)
        ms = [(cb * jnp.exp2(seg[:, q * k:q * (k + 1)])).astype(BF16) for k in range(SSD_HPG)]
        y_pairs = []
        for pr in range(SSD_HPG // 2):
            x_pair = x_g[:, 2 * SSD_HEAD_DIM * pr:2 * SSD_HEAD_DIM * (pr + 1)]
            zero = jnp.zeros_like(x_pair)
            x_bd = jnp.concatenate([jnp.where(low_half, x_pair, zero),
                                    jnp.where(low_half, zero, x_pair)], axis=0)
            m_pair = jnp.concatenate([ms[2 * pr], ms[2 * pr + 1]], axis=1)
            y_pairs.append(jnp.dot(m_pair, x_bd, preferred_element_type=F32))
        y_ref[0, rws, gs] = (y_g + jnp.concatenate(y_pairs, axis=1)).astype(BF16)
        xw = (x_g.astype(F32) * w_rep).astype(BF16)
        s_g = jnp.dot(bt_g, xw, preferred_element_type=F32)
        dec_row = e_rep[q - 1:q, :] if d == 0 else e_rep[0:1, :]
        h_ref[0, d, :, gs] = ht_g * dec_row + s_g

    return group


def _ssd_scan_kernel(xf_ref, xb_ref, btf_ref, btb_ref, dtf_ref, dtb_ref, alog_ref, dskip_ref,
                     tab_ref, indew_ref, *rest, has_h0, chunks):
    if has_h0:
        h0_ref, yf_ref, yb_ref, h_ref, tab_scr = rest
    else:
        yf_ref, yb_ref, h_ref, tab_scr = rest

    @pl.when(pl.program_id(1) == 0)
    def _():
        h_ref[...] = h0_ref[...] if has_h0 else jnp.zeros(h_ref.shape, F32)
        tab_scr[...] = tab_ref[...]

    for sub in range(chunks):
        ro_f = sub * SSD_CHUNK
        ro_b = (chunks - 1 - sub) * SSD_CHUNK
        fwd_group = _scan_dir(0, ro_f, xf_ref, btf_ref, dtf_ref, alog_ref, dskip_ref, tab_scr, indew_ref,
                              yf_ref, h_ref)
        bwd_group = _scan_dir(1, ro_b, xb_ref, btb_ref, dtb_ref, alog_ref, dskip_ref, tab_scr, indew_ref,
                              yb_ref, h_ref)
        for g in range(SSD_GROUPS):
            fwd_group(g)
            bwd_group(g)


def _ssd_scan(xc, bt, dt, a_log, d_skip, h0=None):
    bsz, l, _ = xc.shape
    has_h0 = h0 is not None
    chunks = min(SCAN_CHUNKS, l // SSD_CHUNK)
    rows = chunks * SSD_CHUNK
    nc = l // rows
    fwd = lambda b, c: (b, c, 0)
    bwd = lambda b, c: (b, nc - 1 - c, 0)
    fwd_t = lambda b, c: (b, c, 0, 0)
    bwd_t = lambda b, c: (b, nc - 1 - c, 0, 0)
    st_spec = pl.BlockSpec((1, 2, SSD_STATE, SSD_INNER), lambda b, c: (b, 0, 0, 0))
    seg_tab, ind_ew = _scan_tables()
    return pl.pallas_call(
        functools.partial(_ssd_scan_kernel, has_h0=has_h0, chunks=chunks),
        grid=(bsz, nc),
        in_specs=[
            pl.BlockSpec((1, rows, SSD_INNER + SSD_GN), fwd),
            pl.BlockSpec((1, rows, SSD_INNER + SSD_GN), bwd),
            pl.BlockSpec((1, chunks, SSD_GN, SSD_CHUNK), fwd_t),
            pl.BlockSpec((1, chunks, SSD_GN, SSD_CHUNK), bwd_t),
            pl.BlockSpec((1, rows, LANES), fwd),
            pl.BlockSpec((1, rows, LANES), bwd),
            _const_spec((2, SSD_HEADS)),
            _const_spec((1, SSD_INNER)),
            _const_spec(seg_tab.shape),
            _const_spec(ind_ew.shape),
        ] + ([st_spec] if has_h0 else []),
        out_specs=[
            pl.BlockSpec((1, rows, SSD_INNER), fwd),
            pl.BlockSpec((1, rows, SSD_INNER), bwd),
            st_spec,
        ],
        out_shape=[
            jax.ShapeDtypeStruct((bsz, l, SSD_INNER), BF16),
            jax.ShapeDtypeStruct((bsz, l, SSD_INNER), BF16),
            jax.ShapeDtypeStruct((bsz, 2, SSD_STATE, SSD_INNER), F32),
        ],
        scratch_shapes=[pltpu.VMEM(seg_tab.shape, BF16)],
        compiler_params=_cparams("parallel", "arbitrary"),
        name="ssd_scan",
    )(xc, xc, bt, bt, dt, dt, a_log, d_skip, seg_tab, ind_ew, *([h0] if has_h0 else []))


def _ssd_out_kernel(h_ref, yf_ref, yb_ref, z_ref, mod_ref, g_ref, ng_ref, wo_ref, o_ref):
    gw = SSD_INNER // SSD_GROUPS
    out = None
    for g in range(SSD_GROUPS):
        cols = slice(g * gw, (g + 1) * gw)
        y = (yf_ref[0, :, cols].astype(F32) + yb_ref[0, :, cols].astype(F32)) \
            * _silu(z_ref[0, :, cols].astype(F32))
        y = y * lax.rsqrt(jnp.mean(y * y, axis=-1, keepdims=True) + EPS)
        v = (y * ng_ref[:, cols]).astype(BF16)
        part = jnp.dot(v, wo_ref[cols, :], preferred_element_type=F32)
        out = part if out is None else out + part
    o_ref[0] = h_ref[0] + mod_ref[0, 5:6, :] * _rms(out, g_ref[3:4, :])


def _ssd_out(h, yf, yb, z, mod, g, norm_g, w_out, *, tm, mod_row):
    bsz, l, _ = h.shape
    tok = lambda w: pl.BlockSpec((1, tm, w), lambda b, t: (b, t, 0))
    return pl.pallas_call(
        _ssd_out_kernel,
        grid=(bsz, l // tm),
        in_specs=[
            tok(D_MODEL), tok(SSD_INNER), tok(SSD_INNER), tok(SSD_INNER),
            pl.BlockSpec((1, N_MOD, D_MODEL), lambda b, t: (mod_row(b), 0, 0)),
            _const_spec((6, D_MODEL)),
            _const_spec((1, SSD_INNER)),
            _const_spec((SSD_INNER, D_MODEL)),
        ],
        out_specs=tok(D_MODEL),
        out_shape=jax.ShapeDtypeStruct(h.shape, F32),
        compiler_params=_cparams("parallel", "parallel"),
        name="ssd_out",
    )(h, yf, yb, z, mod, g, norm_g.reshape(1, -1), w_out)


def _gmlp_kernel(h_ref, mod_ref, g_ref, wu_ref, wv_ref, vg_ref, vb_ref, ws_ref, bs_ref, wo_ref, o_ref,
                 p_scr, *, tm):
    rb = min(tm, ROW_BLOCK)
    for r in range(tm // rb):
        blk = slice(r * rb, (r + 1) * rb)
        h = h_ref[0, blk, :]
        u = (_rms(h, g_ref[2:3, :]) * (1.0 + mod_ref[0, 4:5, :]) + mod_ref[0, 3:4, :]).astype(BF16)
        gv = _gelu_tanh(jnp.dot(u, wv_ref[...], preferred_element_type=F32))
        mu = jnp.mean(gv, axis=-1, keepdims=True)
        var = jnp.mean(jnp.square(gv - mu), axis=-1, keepdims=True)
        gvn = ((gv - mu) * lax.rsqrt(var + EPS) * vg_ref[...] + vb_ref[...]).astype(BF16)
        for g in range(GM_GROUPS):
            cols = slice(g * GM_GROUP_DIM, (g + 1) * GM_GROUP_DIM)
            gu = _gelu_tanh(jnp.dot(u, wu_ref[:, cols], preferred_element_type=F32))
            for c in range(rb // GM_CHUNK):
                rows = slice(c * GM_CHUNK, (c + 1) * GM_CHUNK)
                s = jnp.dot(ws_ref[g], gvn[rows, cols], preferred_element_type=F32) + bs_ref[:, cols]
                p_scr[r * rb + c * GM_CHUNK:r * rb + (c + 1) * GM_CHUNK, cols] = (gu[rows, :] * s).astype(BF16)
        y = jnp.dot(p_scr[blk, :], wo_ref[...], preferred_element_type=F32)
        o_ref[0, blk, :] = h + mod_ref[0, 5:6, :] * _rms(y, g_ref[3:4, :])


def _gmlp(h, mod, g, w_in, v_g, v_b, w_s, b_s_full, w_out, *, tm, mod_row):
    bsz, l, _ = h.shape
    resident = dict(pipeline_mode=pl.Buffered(1))
    return pl.pallas_call(
        functools.partial(_gmlp_kernel, tm=tm),
        grid=(bsz, l // tm),
        in_specs=[
            pl.BlockSpec((1, tm, D_MODEL), lambda b, t: (b, t, 0)),
            pl.BlockSpec((1, N_MOD, D_MODEL), lambda b, t: (mod_row(b), 0, 0)),
            _const_spec((6, D_MODEL)),
            pl.BlockSpec((D_MODEL, GM_INNER), lambda b, t: (0, 0), **resident),
            pl.BlockSpec((D_MODEL, GM_INNER), lambda b, t: (0, 1), **resident),
            _const_spec((1, GM_INNER)),
            _const_spec((1, GM_INNER)),
            _const_spec((GM_GROUPS, GM_CHUNK, GM_CHUNK)),
            _const_spec((GM_CHUNK, GM_INNER)),
            pl.BlockSpec((GM_INNER, D_MODEL), lambda b, t: (0, 0), **resident),
        ],
        out_specs=pl.BlockSpec((1, tm, D_MODEL), lambda b, t: (b, t, 0)),
        out_shape=jax.ShapeDtypeStruct(h.shape, F32),
        scratch_shapes=[pltpu.VMEM((tm, GM_INNER), BF16)],
        compiler_params=_cparams("parallel", "parallel"),
        name="gmlp",
    )(h, mod, g, w_in, w_in, v_g.reshape(1, -1), v_b.reshape(1, -1), w_s, b_s_full, w_out)


def kernel(x, c, ctx, c_ctx, ada_w, ada_b, norm_g, ffn_w_in, ffn_w_out, ssd_w_in, ssd_conv_w, ssd_conv_b,
           ssd_dt_bias, ssd_A_log, ssd_D, ssd_norm_g, ssd_w_out, gm_w_in, gm_v_g, gm_v_b, gm_w_s, gm_b_s,
           gm_w_out):
    bsz, seq, _ = x.shape
    ctx_len = ctx.shape[1]
    cvec = jnp.zeros((MOD_ROWS, D_MODEL), F32).at[:bsz].set(c).at[bsz].set(c_ctx)
    mod_all = _ada_mod(cvec, ada_w, ada_b).reshape(DEPTH, MOD_ROWS, N_MOD, D_MODEL)

    x_row = lambda b: b
    ctx_row = lambda b: bsz
    tm_x = 512
    tm_ffn = 1024
    tm_c = ctx_len
    ffn_in = ffn_w_in.astype(BF16)
    ffn_out = ffn_w_out.astype(BF16)

    for i in range(DEPTH):
        use_ssd = (i % 2) == 0
        j = i // 2
        last = i == DEPTH - 1
        ctx_needed = (not last) or use_ssd
        ctx_full = not last
        mod = mod_all[i]
        g = norm_g[i]

        x = _ffn(x, mod, g, ffn_in, ffn_out, layer=i, k=0, s=0, tm=tm_ffn, mod_row=x_row)
        if ctx_needed:
            ctx = _ffn(ctx, mod, g, ffn_in, ffn_out, layer=i, k=0, s=0, tm=tm_c, mod_row=ctx_row)

        if use_ssd:
            w_in = ssd_w_in[j].astype(BF16)
            w_out = ssd_w_out[j].astype(BF16)
            d_skip = jnp.repeat(ssd_D[j], SSD_HEAD_DIM).reshape(1, SSD_INNER)
            dtb = ssd_dt_bias[j].reshape(-1)

            def branch(h, h0, tm, mod_row):
                z, xc, bt, dt = _ssd_in(h, mod, g, w_in, ssd_conv_w[j], ssd_conv_b[j], dtb,
                                        tm=tm, mod_row=mod_row)
                yf, yb, hT = _ssd_scan(xc, bt, dt, ssd_A_log[j], d_skip, h0)
                return yf, yb, z, hT

            cyf, cyb, cz, h_ctx = branch(ctx, None, tm_c, ctx_row)
            xyf, xyb, xz, _ = branch(x, h_ctx, tm_x, x_row)
            x = _ssd_out(x, xyf, xyb, xz, mod, g, ssd_norm_g[j], w_out, tm=tm_x, mod_row=x_row)
            if ctx_full:
                ctx = _ssd_out(ctx, cyf, cyb, cz, mod, g, ssd_norm_g[j], w_out, tm=tm_c, mod_row=ctx_row)
        else:
            gw_in = gm_w_in[j].astype(BF16)
            gw_out = gm_w_out[j].astype(BF16)
            ws = gm_w_s[j].astype(BF16)
            bs_full = jnp.repeat(gm_b_s[j].T, GM_GROUP_DIM, axis=1)
            x = _gmlp(x, mod, g, gw_in, gm_v_g[j], gm_v_b[j], ws, bs_full, gw_out, tm=tm_ffn, mod_row=x_row)
            if ctx_full:
                ctx = _gmlp(ctx, mod, g, gw_in, gm_v_g[j], gm_v_b[j], ws, bs_full, gw_out,
                            tm=tm_c, mod_row=ctx_row)

        x = _ffn(x, mod, g, ffn_in, ffn_out, layer=i, k=1, s=2, tm=tm_ffn, mod_row=x_row)
        if ctx_full:
            ctx = _ffn(ctx, mod, g, ffn_in, ffn_out, layer=i, k=1, s=2, tm=tm_c, mod_row=ctx_row)
    return x
```

```python
import functools
import math

import jax
import jax.numpy as jnp
from jax import lax
from jax.experimental import pallas as pl
from jax.experimental.pallas import tpu as pltpu

F32 = jnp.float32
BF16 = jnp.bfloat16

D_MODEL = 1024
DEPTH = 2
N_MOD = 9
MACARON_W = 0.5
EPS = 1e-6
FFN_DIM = 2816

SSD_INNER = 2048
SSD_HEAD_DIM = 64
SSD_HEADS = 32
SSD_GROUPS = 8
SSD_HPG = 4
SSD_STATE = 128
SSD_CONV = 5
SSD_CHUNK = 128
SSD_GN = SSD_GROUPS * SSD_STATE
SSD_CONV_DIM = SSD_INNER + 2 * SSD_GN

GM_CHUNK = 128
GM_INNER = 2048
GM_GROUPS = 8
GM_GROUP_DIM = 256

MXU_TILE = 256
LANES = 128
SUBLANES = 8
BF16_ROWS = 16
SCAN_CHUNKS = 4
ROW_BLOCK = 512
HALO = 16
VMEM_LIMIT = 56 * 1024 * 1024
MOD_ROWS = 16
LOG2E = math.log2(math.e)
NEG_BIG = -1e30


def _cparams(*sem):
    return pltpu.CompilerParams(dimension_semantics=sem, vmem_limit_bytes=VMEM_LIMIT)


def _rms(x, g):
    return x * lax.rsqrt(jnp.mean(x * x, axis=-1, keepdims=True) + EPS) * g


def _silu(x):
    return x * jax.nn.sigmoid(x)


def _gelu_tanh(x):
    k = -2.0 * LOG2E * math.sqrt(2.0 / math.pi)
    t = x * (k + (k * 0.044715) * (x * x))
    return x / (1.0 + jnp.exp2(t))


def _const_spec(shape):
    nd = len(shape)
    return pl.BlockSpec(shape, lambda *_: (0,) * nd)


def _ada_kernel(c_ref, w_ref, b_ref, o_ref):
    cs = _silu(c_ref[...]).astype(BF16)
    o_ref[...] = jnp.dot(cs, w_ref[...].astype(BF16), preferred_element_type=F32) + b_ref[...]


def _ada_mod(cvec, ada_w, ada_b):
    nd = N_MOD * D_MODEL
    bn = 1024
    return pl.pallas_call(
        _ada_kernel,
        grid=(DEPTH, nd // bn),
        in_specs=[
            pl.BlockSpec((MOD_ROWS, D_MODEL), lambda i, j: (0, 0)),
            pl.BlockSpec((None, D_MODEL, bn), lambda i, j: (i, 0, j)),
            pl.BlockSpec((None, 1, bn), lambda i, j: (i, 0, j)),
        ],
        out_specs=pl.BlockSpec((None, MOD_ROWS, bn), lambda i, j: (i, 0, j)),
        out_shape=jax.ShapeDtypeStruct((DEPTH, MOD_ROWS, nd), F32),
        compiler_params=_cparams("parallel", "parallel"),
        name="ada_mod",
    )(cvec, ada_w, ada_b.reshape(DEPTH, 1, nd))


def _ffn_kernel(h_ref, mod_ref, g_ref, wg_ref, wu_ref, wo_ref, o_ref, a_scr, *, s, tm):
    shift = mod_ref[0, 3 * s:3 * s + 1, :]
    scale = mod_ref[0, 3 * s + 1:3 * s + 2, :]
    gate = mod_ref[0, 3 * s + 2:3 * s + 3, :]
    rb = min(tm, ROW_BLOCK)
    for r in range(tm // rb):
        rows = slice(r * rb, (r + 1) * rb)
        h = h_ref[0, rows, :]
        u = (_rms(h, g_ref[2 * s:2 * s + 1, :]) * (1.0 + scale) + shift).astype(BF16)
        for c in range(FFN_DIM // MXU_TILE):
            cols = slice(c * MXU_TILE, (c + 1) * MXU_TILE)
            hg = jnp.dot(u, wg_ref[:, cols], preferred_element_type=F32)
            hu = jnp.dot(u, wu_ref[:, cols], preferred_element_type=F32)
            a_scr[rows, cols] = (_silu(hg) * hu).astype(BF16)
        y = jnp.dot(a_scr[rows, :], wo_ref[...], preferred_element_type=F32)
        o_ref[0, rows, :] = h + (MACARON_W * gate) * _rms(y, g_ref[2 * s + 1:2 * s + 2, :])


def _ffn(h, mod, g, w_in, w_out, *, layer, k, s, tm, mod_row):
    bsz, l, _ = h.shape
    resident = dict(pipeline_mode=pl.Buffered(1))
    return pl.pallas_call(
        functools.partial(_ffn_kernel, s=s, tm=tm),
        grid=(bsz, l // tm),
        in_specs=[
            pl.BlockSpec((1, tm, D_MODEL), lambda b, t: (b, t, 0)),
            pl.BlockSpec((1, N_MOD, D_MODEL), lambda b, t: (mod_row(b), 0, 0)),
            _const_spec((6, D_MODEL)),
            pl.BlockSpec((None, None, D_MODEL, FFN_DIM), lambda b, t: (layer, k, 0, 0), **resident),
            pl.BlockSpec((None, None, D_MODEL, FFN_DIM), lambda b, t: (layer, k, 0, 1), **resident),
            pl.BlockSpec((None, None, FFN_DIM, D_MODEL), lambda b, t: (layer, k, 0, 0), **resident),
        ],
        out_specs=pl.BlockSpec((1, tm, D_MODEL), lambda b, t: (b, t, 0)),
        out_shape=jax.ShapeDtypeStruct(h.shape, F32),
        scratch_shapes=[pltpu.VMEM((tm, FFN_DIM), BF16)],
        compiler_params=_cparams("parallel", "parallel"),
        name="ffn",
    )(h, mod, g, w_in, w_in, w_out)


def _ssd_in_kernel(xm_ref, xp_ref, xn_ref, mod_ref, g_ref, wz_ref, wxa_ref, wxb_ref, wdt_ref, cw_ref, cb_ref,
                   dtb_ref, z_ref, xc_ref, bt_ref, dt_ref, u_scr, p_scr, o_scr, *, tm):
    t = pl.program_id(1)
    nt = pl.num_programs(1)
    shift = mod_ref[0, 3:4, :]
    scale = mod_ref[0, 4:5, :]
    g_pre = g_ref[2:3, :]

    def prep(h):
        return _rms(h, g_pre) * (1.0 + scale) + shift

    u_scr[0:HALO, :] = jnp.where(t == 0, 0.0, prep(xp_ref[0])).astype(BF16)
    u_scr[HALO:HALO + tm, :] = prep(xm_ref[0]).astype(BF16)
    u_scr[HALO + tm:, :] = jnp.where(t == nt - 1, 0.0, prep(xn_ref[0])).astype(BF16)
    um = u_scr[HALO:HALO + tm, :]
    dt_raw = jnp.dot(um, wdt_ref[...], preferred_element_type=F32) + dtb_ref[...]
    dt_ref[0] = jnp.maximum(dt_raw, 0.0) + jnp.log1p(jnp.exp(-jnp.abs(dt_raw)))

    nblk = SSD_CONV_DIM // LANES
    npair = nblk // 2
    pairs_a = SSD_INNER // MXU_TILE
    groups8 = tm // (SUBLANES * SUBLANES)

    def project(pair):
        if pair < pairs_a:
            w = wxa_ref[:, pair * MXU_TILE:(pair + 1) * MXU_TILE]
        else:
            w = wxb_ref[:, (pair - pairs_a) * MXU_TILE:(pair - pairs_a + 1) * MXU_TILE]
        p = jnp.dot(u_scr[...], w, preferred_element_type=F32)
        p_scr[2 * pair] = p[:, :LANES]
        p_scr[2 * pair + 1] = p[:, LANES:]

    def conv_block(blk):
        cols = slice(blk * LANES, (blk + 1) * LANES)
        wk = [jnp.broadcast_to(cw_ref[k:k + 1, cols], (SUBLANES, LANES)) for k in range(SSD_CONV)]
        bias = jnp.broadcast_to(cb_ref[:, cols], (SUBLANES, LANES))
        slot = blk % 2
        for m in range(groups8):
            base = HALO - SSD_CONV // 2 + SUBLANES * SUBLANES * m
            slabs = [p_scr[blk, pl.ds(base + r, SUBLANES, stride=SUBLANES), :]
                     for r in range(SSD_CONV - 1)]
            for s in range(SUBLANES):
                slabs.append(p_scr[blk, pl.ds(base + s + SSD_CONV - 1, SUBLANES, stride=SUBLANES), :])
                acc = bias
                for k in range(SSD_CONV):
                    acc = acc + wk[k] * slabs[s + k]
                o_scr[slot, pl.ds(SUBLANES * SUBLANES * m + s, SUBLANES, stride=SUBLANES), :] = _silu(acc)
        val = o_scr[slot]
        nx = SSD_INNER // LANES
        nb = SSD_GN // LANES
        if blk < nx:
            xc_ref[0, :, cols] = val.astype(BF16)
        elif blk < nx + nb:
            rows = slice((blk - nx) * LANES, (blk - nx + 1) * LANES)
            val_t = val.T.astype(BF16)
            for ck in range(tm // SSD_CHUNK):
                bt_ref[0, ck, rows, :] = val_t[:, ck * SSD_CHUNK:(ck + 1) * SSD_CHUNK]
        else:
            oc = slice((blk - nb) * LANES, (blk - nb + 1) * LANES)
            xc_ref[0, :, oc] = val.astype(BF16)

    project(0)
    z_ref[0] = jnp.dot(um, wz_ref[...], preferred_element_type=F32).astype(BF16)
    for pair in range(npair):
        conv_block(2 * pair)
        if pair + 1 < npair:
            project(pair + 1)
        conv_block(2 * pair + 1)


def _ssd_in(h, mod, g, w_in, conv_w, conv_b, dt_bias, *, tm, mod_row):
    bsz, l, _ = h.shape
    nh = tm // HALO
    last = l // HALO - 1
    dt_pad = LANES - 2 * SSD_HEADS
    wdt = jnp.pad(w_in[:, SSD_INNER + SSD_CONV_DIM:], ((0, 0), (0, dt_pad)))
    dt_bias = jnp.pad(dt_bias, (0, dt_pad))
    wblk = lambda j: pl.BlockSpec((D_MODEL, SSD_INNER), lambda b, t: (0, j))
    return pl.pallas_call(
        functools.partial(_ssd_in_kernel, tm=tm),
        grid=(bsz, l // tm),
        in_specs=[
            pl.BlockSpec((1, tm, D_MODEL), lambda b, t: (b, t, 0)),
            pl.BlockSpec((1, HALO, D_MODEL), lambda b, t: (b, jnp.maximum(t * nh - 1, 0), 0)),
            pl.BlockSpec((1, HALO, D_MODEL), lambda b, t: (b, jnp.minimum((t + 1) * nh, last), 0)),
            pl.BlockSpec((1, N_MOD, D_MODEL), lambda b, t: (mod_row(b), 0, 0)),
            _const_spec((6, D_MODEL)),
            wblk(0), wblk(1), wblk(2),
            _const_spec((D_MODEL, LANES)),
            _const_spec((SSD_CONV, SSD_CONV_DIM)),
            _const_spec((1, SSD_CONV_DIM)),
            _const_spec((1, LANES)),
        ],
        out_specs=[
            pl.BlockSpec((1, tm, SSD_INNER), lambda b, t: (b, t, 0)),
            pl.BlockSpec((1, tm, SSD_INNER + SSD_GN), lambda b, t: (b, t, 0)),
            pl.BlockSpec((1, tm // SSD_CHUNK, SSD_GN, SSD_CHUNK), lambda b, t: (b, t, 0, 0)),
            pl.BlockSpec((1, tm, LANES), lambda b, t: (b, t, 0)),
        ],
        out_shape=[
            jax.ShapeDtypeStruct((bsz, l, SSD_INNER), BF16),
            jax.ShapeDtypeStruct((bsz, l, SSD_INNER + SSD_GN), BF16),
            jax.ShapeDtypeStruct((bsz, l // SSD_CHUNK, SSD_GN, SSD_CHUNK), BF16),
            jax.ShapeDtypeStruct((bsz, l, LANES), F32),
        ],
        scratch_shapes=[
            pltpu.VMEM((tm + 2 * HALO, D_MODEL), BF16),
            pltpu.VMEM((SSD_CONV_DIM // LANES, tm + 2 * HALO, LANES), F32),
            pltpu.VMEM((2, tm, LANES), F32),
        ],
        compiler_params=_cparams("parallel", "parallel"),
        name="ssd_in",
    )(h, h, h, mod, g, w_in, w_in, w_in, wdt, conv_w, conv_b.reshape(1, -1), dt_bias.reshape(1, -1))


def _split_bf16(v, n):
    parts = []
    for _ in range(n):
        p = v.astype(BF16).astype(F32)
        parts.append(p)
        v = v - p
    return parts


def _scan_tables():
    q = SSD_CHUNK
    r = jnp.arange(LANES)[:, None]
    ca = jnp.arange(SSD_HEADS * q)[None, :]
    ind_a = (((r % SSD_HEADS) == (ca // q)) & (r < 3 * SSD_HEADS)).astype(F32)
    j = ca % q
    mask_f = jnp.where(j > r, NEG_BIG, 0.0)
    mask_b = jnp.where(j < r, NEG_BIG, 0.0)
    seg_tab = jnp.stack([jnp.concatenate([ind_a, mask_f], axis=0),
                         jnp.concatenate([ind_a, mask_b], axis=0)]).astype(BF16)
    ce = jnp.arange(2 * SSD_INNER)[None, :]
    ind_ew = ((r % SSD_HEADS) == ((ce % SSD_INNER) // SSD_HEAD_DIM)) \
        & ((r // (2 * SSD_HEADS)) == (ce // SSD_INNER))
    return seg_tab, ind_ew.astype(BF16)


def _scan_dir(d, ro, xc_ref, bt_ref, dt_ref, alog_ref, dskip_ref, tab_scr, indew_ref, y_ref, h_ref):
    q = SSD_CHUNK
    nh = SSD_HEADS
    rws = slice(ro, ro + q)
    dt = dt_ref[0, rws, nh * d:nh * (d + 1)]
    a_neg = -jnp.exp(alog_ref[d:d + 1, :])
    row = lax.broadcasted_iota(jnp.int32, (q, q), 0)
    col = lax.broadcasted_iota(jnp.int32, (q, q), 1)
    tri = (row >= col) if d == 0 else (row <= col)
    pad_a = jnp.zeros((q, LANES - 3 * nh), F32)
    cs = jnp.dot(tri.astype(BF16),
                 jnp.concatenate(_split_bf16(dt * a_neg, 3) + [pad_a], axis=1).astype(BF16),
                 preferred_element_type=F32)
    a = cs[:, 0:nh] + cs[:, nh:2 * nh] + cs[:, 2 * nh:3 * nh]
    a_end = a[q - 1:q, :] if d == 0 else a[0:1, :]
    e_in = jnp.exp(a)
    w_end = dt * jnp.exp(a_end - a)
    r_nat = LOG2E * (jnp.maximum(jnp.log(dt), NEG_BIG) - a)
    r_t = jnp.concatenate([r_nat, jnp.zeros((q, LANES - nh), F32)], axis=1).T[0:nh]
    r_parts = _split_bf16(r_t, 3)
    prow = lax.broadcasted_iota(jnp.int32, (BF16_ROWS, q), 0)
    for hd in range(nh):
        tile = jnp.where(prow == 0, r_parts[0][hd:hd + 1, :],
                         jnp.where(prow == 1, r_parts[1][hd:hd + 1, :],
                                   jnp.where(prow == 2, r_parts[2][hd:hd + 1, :], 0.0)))
        tab_scr[d, 3 * nh:3 * nh + BF16_ROWS, q * hd:q * (hd + 1)] = tile.astype(BF16)
    ones3 = (lax.broadcasted_iota(jnp.int32, (q, LANES - 3 * nh), 1) < 3).astype(F32)
    eye = (row == col).astype(F32)
    seg_lhs = jnp.concatenate(_split_bf16(LOG2E * a, 3) + [ones3, eye], axis=1).astype(BF16)
    ew_parts = jnp.concatenate(_split_bf16(e_in, 2) + _split_bf16(w_end, 2), axis=1).astype(BF16)
    low_half = lax.broadcasted_iota(jnp.int32, (q, 2 * SSD_HEAD_DIM), 1) < SSD_HEAD_DIM

    gw = SSD_HPG * SSD_HEAD_DIM

    def group(g):
        gs = slice(gw * g, gw * (g + 1))
        c_g = xc_ref[0, rws, SSD_INNER + SSD_STATE * g:SSD_INNER + SSD_STATE * (g + 1)]
        bt_g = bt_ref[0, ro // q, SSD_STATE * g:SSD_STATE * (g + 1), :]
        x_g = xc_ref[0, rws, gs]
        ht_g = h_ref[0, d, :, gs]
        cb = jnp.dot(c_g, bt_g, preferred_element_type=F32)
        seg = jnp.dot(seg_lhs, tab_scr[d, :, SSD_HPG * q * g:SSD_HPG * q * (g + 1)],
                      preferred_element_type=F32)
        e_rep = jnp.dot(ew_parts, indew_ref[:, gs], preferred_element_type=F32)
        w_rep = jnp.dot(ew_parts, indew_ref[:, SSD_INNER + gw * g:SSD_INNER + gw * (g + 1)],
                        preferred_element_type=F32)
        y_g = jnp.dot(c_g, ht_g.astype(BF16), preferred_element_type=F32) * e_rep
        if d == 0:
            y_g = y_g + dskip_ref[:, gs] * x_g.astype(F32)
        ms = [(cb * jnp.exp2(seg[:, q * k:q * (k + 1)])).astype(BF16) for k in range(SSD_HPG)]
        y_pairs = []
        for pr in range(SSD_HPG // 2):
            x_pair = x_g[:, 2 * SSD_HEAD_DIM * pr:2 * SSD_HEAD_DIM * (pr + 1)]
            zero = jnp.zeros_like(x_pair)
            x_bd = jnp.concatenate([jnp.where(low_half, x_pair, zero),
                                    jnp.where(low_half, zero, x_pair)], axis=0)
            m_pair = jnp.concatenate([ms[2 * pr], ms[2 * pr + 1]], axis=1)
            y_pairs.append(jnp.dot(m_pair, x_bd, preferred_element_type=F32))
        y_ref[0, rws, gs] = (y_g + jnp.concatenate(y_pairs, axis=1)).astype(BF16)
        xw = (x_g.astype(F32) * w_rep).astype(BF16)
        s_g = jnp.dot(bt_g, xw, preferred_element_type=F32)
        dec_row = e_rep[q - 1:q, :] if d == 0 else e_rep[0:1, :]
        h_ref[0, d, :, gs] = ht_g * dec_row + s_g

    return group


def _ssd_scan_kernel(xf_ref, xb_ref, btf_ref, btb_ref, dtf_ref, dtb_ref, alog_ref, dskip_ref,
                     tab_ref, indew_ref, *rest, has_h0, chunks):
    if has_h0:
        h0_ref, yf_ref, yb_ref, h_ref, tab_scr = rest
    else:
        yf_ref, yb_ref, h_ref, tab_scr = rest

    @pl.when(pl.program_id(1) == 0)
    def _():
        h_ref[...] = h0_ref[...] if has_h0 else jnp.zeros(h_ref.shape, F32)
        tab_scr[...] = tab_ref[...]

    for sub in range(chunks):
        ro_f = sub * SSD_CHUNK
        ro_b = (chunks - 1 - sub) * SSD_CHUNK
        fwd_group = _scan_dir(0, ro_f, xf_ref, btf_ref, dtf_ref, alog_ref, dskip_ref, tab_scr, indew_ref,
                              yf_ref, h_ref)
        bwd_group = _scan_dir(1, ro_b, xb_ref, btb_ref, dtb_ref, alog_ref, dskip_ref, tab_scr, indew_ref,
                              yb_ref, h_ref)
        for g in range(SSD_GROUPS):
            fwd_group(g)
            bwd_group(g)


def _ssd_scan(xc, bt, dt, a_log, d_skip, h0=None):
    bsz, l, _ = xc.shape
    has_h0 = h0 is not None
    chunks = min(SCAN_CHUNKS, l // SSD_CHUNK)
    rows = chunks * SSD_CHUNK
    nc = l // rows
    fwd = lambda b, c: (b, c, 0)
    bwd = lambda b, c: (b, nc - 1 - c, 0)
    fwd_t = lambda b, c: (b, c, 0, 0)
    bwd_t = lambda b, c: (b, nc - 1 - c, 0, 0)
    st_spec = pl.BlockSpec((1, 2, SSD_STATE, SSD_INNER), lambda b, c: (b, 0, 0, 0))
    seg_tab, ind_ew = _scan_tables()
    return pl.pallas_call(
        functools.partial(_ssd_scan_kernel, has_h0=has_h0, chunks=chunks),
        grid=(bsz, nc),
        in_specs=[
            pl.BlockSpec((1, rows, SSD_INNER + SSD_GN), fwd),
            pl.BlockSpec((1, rows, SSD_INNER + SSD_GN), bwd),
            pl.BlockSpec((1, chunks, SSD_GN, SSD_CHUNK), fwd_t),
            pl.BlockSpec((1, chunks, SSD_GN, SSD_CHUNK), bwd_t),
            pl.BlockSpec((1, rows, LANES), fwd),
            pl.BlockSpec((1, rows, LANES), bwd),
            _const_spec((2, SSD_HEADS)),
            _const_spec((1, SSD_INNER)),
            _const_spec(seg_tab.shape),
            _const_spec(ind_ew.shape),
        ] + ([st_spec] if has_h0 else []),
        out_specs=[
            pl.BlockSpec((1, rows, SSD_INNER), fwd),
            pl.BlockSpec((1, rows, SSD_INNER), bwd),
            st_spec,
        ],
        out_shape=[
            jax.ShapeDtypeStruct((bsz, l, SSD_INNER), BF16),
            jax.ShapeDtypeStruct((bsz, l, SSD_INNER), BF16),
            jax.ShapeDtypeStruct((bsz, 2, SSD_STATE, SSD_INNER), F32),
        ],
        scratch_shapes=[pltpu.VMEM(seg_tab.shape, BF16)],
        compiler_params=_cparams("parallel", "arbitrary"),
        name="ssd_scan",
    )(xc, xc, bt, bt, dt, dt, a_log, d_skip, seg_tab, ind_ew, *([h0] if has_h0 else []))


def _ssd_out_kernel(h_ref, yf_ref, yb_ref, z_ref, mod_ref, g_ref, ng_ref, wo_ref, o_ref):
    gw = SSD_INNER // SSD_GROUPS
    out = None
    for g in range(SSD_GROUPS):
        cols = slice(g * gw, (g + 1) * gw)
        y = (yf_ref[0, :, cols].astype(F32) + yb_ref[0, :, cols].astype(F32)) \
            * _silu(z_ref[0, :, cols].astype(F32))
        y = y * lax.rsqrt(jnp.mean(y * y, axis=-1, keepdims=True) + EPS)
        v = (y * ng_ref[:, cols]).astype(BF16)
        part = jnp.dot(v, wo_ref[cols, :], preferred_element_type=F32)
        out = part if out is None else out + part
    o_ref[0] = h_ref[0] + mod_ref[0, 5:6, :] * _rms(out, g_ref[3:4, :])


def _ssd_out(h, yf, yb, z, mod, g, norm_g, w_out, *, tm, mod_row):
    bsz, l, _ = h.shape
    tok = lambda w: pl.BlockSpec((1, tm, w), lambda b, t: (b, t, 0))
    return pl.pallas_call(
        _ssd_out_kernel,
        grid=(bsz, l // tm),
        in_specs=[
            tok(D_MODEL), tok(SSD_INNER), tok(SSD_INNER), tok(SSD_INNER),
            pl.BlockSpec((1, N_MOD, D_MODEL), lambda b, t: (mod_row(b), 0, 0)),
            _const_spec((6, D_MODEL)),
            _const_spec((1, SSD_INNER)),
            _const_spec((SSD_INNER, D_MODEL)),
        ],
        out_specs=tok(D_MODEL),
        out_shape=jax.ShapeDtypeStruct(h.shape, F32),
        compiler_params=_cparams("parallel", "parallel"),
        name="ssd_out",
    )(h, yf, yb, z, mod, g, norm_g.reshape(1, -1), w_out)


def _ssd_out_ffn_kernel(h_ref, yf_ref, yb_ref, z_ref, mod_ref, g_ref, ng_ref, wo_ref, wg_ref, wu_ref, wf_ref,
                        o_ref, a_scr):
    gw = SSD_INNER // SSD_GROUPS
    out = None
    for g in range(SSD_GROUPS):
        cols = slice(g * gw, (g + 1) * gw)
        y = (yf_ref[0, :, cols].astype(F32) + yb_ref[0, :, cols].astype(F32)) \
            * _silu(z_ref[0, :, cols].astype(F32))
        y = y * lax.rsqrt(jnp.mean(y * y, axis=-1, keepdims=True) + EPS)
        v = (y * ng_ref[:, cols]).astype(BF16)
        part = jnp.dot(v, wo_ref[cols, :], preferred_element_type=F32)
        out = part if out is None else out + part
    h = h_ref[0] + mod_ref[0, 5:6, :] * _rms(out, g_ref[3:4, :])
    u = (_rms(h, g_ref[4:5, :]) * (1.0 + mod_ref[0, 7:8, :]) + mod_ref[0, 6:7, :]).astype(BF16)
    for c in range(FFN_DIM // MXU_TILE):
        cols = slice(c * MXU_TILE, (c + 1) * MXU_TILE)
        hg = jnp.dot(u, wg_ref[:, cols], preferred_element_type=F32)
        hu = jnp.dot(u, wu_ref[:, cols], preferred_element_type=F32)
        a_scr[:, cols] = (_silu(hg) * hu).astype(BF16)
    y2 = jnp.dot(a_scr[...], wf_ref[...], preferred_element_type=F32)
    o_ref[0] = h + (MACARON_W * mod_ref[0, 8:9, :]) * _rms(y2, g_ref[5:6, :])


def _ssd_out_ffn(h, yf, yb, z, mod, g, norm_g, w_out, ffn_in, ffn_out, *, layer, tm, mod_row):
    bsz, l, _ = h.shape
    tok = lambda w: pl.BlockSpec((1, tm, w), lambda b, t: (b, t, 0))
    resident = dict(pipeline_mode=pl.Buffered(1))
    return pl.pallas_call(
        _ssd_out_ffn_kernel,
        grid=(bsz, l // tm),
        in_specs=[
            tok(D_MODEL), tok(SSD_INNER), tok(SSD_INNER), tok(SSD_INNER),
            pl.BlockSpec((1, N_MOD, D_MODEL), lambda b, t: (mod_row(b), 0, 0)),
            _const_spec((6, D_MODEL)),
            _const_spec((1, SSD_INNER)),
            pl.BlockSpec((SSD_INNER, D_MODEL), lambda b, t: (0, 0), **resident),
            pl.BlockSpec((None, None, D_MODEL, FFN_DIM), lambda b, t: (layer, 1, 0, 0), **resident),
            pl.BlockSpec((None, None, D_MODEL, FFN_DIM), lambda b, t: (layer, 1, 0, 1), **resident),
            pl.BlockSpec((None, None, FFN_DIM, D_MODEL), lambda b, t: (layer, 1, 0, 0), **resident),
        ],
        out_specs=tok(D_MODEL),
        out_shape=jax.ShapeDtypeStruct(h.shape, F32),
        scratch_shapes=[pltpu.VMEM((tm, FFN_DIM), BF16)],
        compiler_params=_cparams("parallel", "parallel"),
        name="ssd_out_ffn",
    )(h, yf, yb, z, mod, g, norm_g.reshape(1, -1), w_out, ffn_in, ffn_in, ffn_out)


def _gmlp_kernel(h_ref, mod_ref, g_ref, wu_ref, wv_ref, vg_ref, vb_ref, ws_ref, bs_ref, wo_ref, o_ref,
                 p_scr, *, tm):
    rb = min(tm, ROW_BLOCK)
    for r in range(tm // rb):
        blk = slice(r * rb, (r + 1) * rb)
        h = h_ref[0, blk, :]
        u = (_rms(h, g_ref[2:3, :]) * (1.0 + mod_ref[0, 4:5, :]) + mod_ref[0, 3:4, :]).astype(BF16)
        gv = _gelu_tanh(jnp.dot(u, wv_ref[...], preferred_element_type=F32))
        mu = jnp.mean(gv, axis=-1, keepdims=True)
        var = jnp.mean(jnp.square(gv - mu), axis=-1, keepdims=True)
        gvn = ((gv - mu) * lax.rsqrt(var + EPS) * vg_ref[...] + vb_ref[...]).astype(BF16)
        for g in range(GM_GROUPS):
            cols = slice(g * GM_GROUP_DIM, (g + 1) * GM_GROUP_DIM)
            gu = _gelu_tanh(jnp.dot(u, wu_ref[:, cols], preferred_element_type=F32))
            for c in range(rb // GM_CHUNK):
                rows = slice(c * GM_CHUNK, (c + 1) * GM_CHUNK)
                s = jnp.dot(ws_ref[g], gvn[rows, cols], preferred_element_type=F32) + bs_ref[:, cols]
                p_scr[r * rb + c * GM_CHUNK:r * rb + (c + 1) * GM_CHUNK, cols] = (gu[rows, :] * s).astype(BF16)
        y = jnp.dot(p_scr[blk, :], wo_ref[...], preferred_element_type=F32)
        o_ref[0, blk, :] = h + mod_ref[0, 5:6, :] * _rms(y, g_ref[3:4, :])


def _gmlp(h, mod, g, w_in, v_g, v_b, w_s, b_s_full, w_out, *, tm, mod_row):
    bsz, l, _ = h.shape
    resident = dict(pipeline_mode=pl.Buffered(1))
    return pl.pallas_call(
        functools.partial(_gmlp_kernel, tm=tm),
        grid=(bsz, l // tm),
        in_specs=[
            pl.BlockSpec((1, tm, D_MODEL), lambda b, t: (b, t, 0)),
            pl.BlockSpec((1, N_MOD, D_MODEL), lambda b, t: (mod_row(b), 0, 0)),
            _const_spec((6, D_MODEL)),
            pl.BlockSpec((D_MODEL, GM_INNER), lambda b, t: (0, 0), **resident),
            pl.BlockSpec((D_MODEL, GM_INNER), lambda b, t: (0, 1), **resident),
            _const_spec((1, GM_INNER)),
            _const_spec((1, GM_INNER)),
            _const_spec((GM_GROUPS, GM_CHUNK, GM_CHUNK)),
            _const_spec((GM_CHUNK, GM_INNER)),
            pl.BlockSpec((GM_INNER, D_MODEL), lambda b, t: (0, 0), **resident),
        ],
        out_specs=pl.BlockSpec((1, tm, D_MODEL), lambda b, t: (b, t, 0)),
        out_shape=jax.ShapeDtypeStruct(h.shape, F32),
        scratch_shapes=[pltpu.VMEM((tm, GM_INNER), BF16)],
        compiler_params=_cparams("parallel", "parallel"),
        name="gmlp",
    )(h, mod, g, w_in, w_in, v_g.reshape(1, -1), v_b.reshape(1, -1), w_s, b_s_full, w_out)


def kernel(x, c, ctx, c_ctx, ada_w, ada_b, norm_g, ffn_w_in, ffn_w_out, ssd_w_in, ssd_conv_w, ssd_conv_b,
           ssd_dt_bias, ssd_A_log, ssd_D, ssd_norm_g, ssd_w_out, gm_w_in, gm_v_g, gm_v_b, gm_w_s, gm_b_s,
           gm_w_out):
    bsz, seq, _ = x.shape
    ctx_len = ctx.shape[1]
    cvec = jnp.zeros((MOD_ROWS, D_MODEL), F32).at[:bsz].set(c).at[bsz].set(c_ctx)
    mod_all = _ada_mod(cvec, ada_w, ada_b).reshape(DEPTH, MOD_ROWS, N_MOD, D_MODEL)

    x_row = lambda b: b
    ctx_row = lambda b: bsz
    tm_x = 512
    tm_ffn = 1024
    tm_c = ctx_len
    ffn_in = ffn_w_in.astype(BF16)
    ffn_out = ffn_w_out.astype(BF16)

    for i in range(DEPTH):
        use_ssd = (i % 2) == 0
        x_ffn2_done = False
        j = i // 2
        last = i == DEPTH - 1
        ctx_needed = (not last) or use_ssd
        ctx_full = not last
        mod = mod_all[i]
        g = norm_g[i]

        x = _ffn(x, mod, g, ffn_in, ffn_out, layer=i, k=0, s=0, tm=tm_ffn, mod_row=x_row)
        if ctx_needed:
            ctx = _ffn(ctx, mod, g, ffn_in, ffn_out, layer=i, k=0, s=0, tm=tm_c, mod_row=ctx_row)

        if use_ssd:
            w_in = ssd_w_in[j].astype(BF16)
            w_out = ssd_w_out[j].astype(BF16)
            d_skip = jnp.repeat(ssd_D[j], SSD_HEAD_DIM).reshape(1, SSD_INNER)
            dtb = ssd_dt_bias[j].reshape(-1)

            def branch(h, h0, tm, mod_row):
                z, xc, bt, dt = _ssd_in(h, mod, g, w_in, ssd_conv_w[j], ssd_conv_b[j], dtb,
                                        tm=tm, mod_row=mod_row)
                yf, yb, hT = _ssd_scan(xc, bt, dt, ssd_A_log[j], d_skip, h0)
                return yf, yb, z, hT

            cyf, cyb, cz, h_ctx = branch(ctx, None, tm_c, ctx_row)
            xyf, xyb, xz, _ = branch(x, h_ctx, tm_x, x_row)
            x = _ssd_out_ffn(x, xyf, xyb, xz, mod, g, ssd_norm_g[j], w_out, ffn_in, ffn_out, layer=i,
                             tm=tm_x, mod_row=x_row)
            x_ffn2_done = True
            if ctx_full:
                ctx = _ssd_out(ctx, cyf, cyb, cz, mod, g, ssd_norm_g[j], w_out, tm=tm_c, mod_row=ctx_row)
        else:
            gw_in = gm_w_in[j].astype(BF16)
            gw_out = gm_w_out[j].astype(BF16)
            ws = gm_w_s[j].astype(BF16)
            bs_full = jnp.repeat(gm_b_s[j].T, GM_GROUP_DIM, axis=1)
            x = _gmlp(x, mod, g, gw_in, gm_v_g[j], gm_v_b[j], ws, bs_full, gw_out, tm=tm_ffn, mod_row=x_row)
            if ctx_full:
                ctx = _gmlp(ctx, mod, g, gw_in, gm_v_g[j], gm_v_b[j], ws, bs_full, gw_out,
                            tm=tm_c, mod_row=ctx_row)

        if not x_ffn2_done:
            x = _ffn(x, mod, g, ffn_in, ffn_out, layer=i, k=1, s=2, tm=tm_ffn, mod_row=x_row)
        if ctx_full:
            ctx = _ffn(ctx, mod, g, ffn_in, ffn_out, layer=i, k=1, s=2, tm=tm_c, mod_row=ctx_row)
    return x
```
